```python
import jax, jax.numpy as jnp
from jax import lax
import numpy as np

D_MODEL = 1024
BATCH = 32
SEQ = 2048
DEPTH = 1

FOX_HEADS = 8
FOX_HEAD_DIM = 64
FOX_WIDTH = FOX_HEADS * FOX_HEAD_DIM
RET_HEADS = 4
RET_HEAD_DIM = 128
RET_WIDTH = RET_HEADS * RET_HEAD_DIM
MIX_WIDTH = FOX_WIDTH + RET_WIDTH
D_FF = 4 * D_MODEL
Q_BLOCK = 128
RET_CHUNK = 128
ROPE_BASE = 10000.0
EPS = 1e-6
N_MOD = 6
IN_COLS = 4 * FOX_WIDTH + FOX_HEADS + 4 * RET_WIDTH

kernel_name = "hybrid_fox_retention_adaln_block"


def _rms(x):
    x32 = x.astype(jnp.float32)
    return x32 * lax.rsqrt(jnp.mean(x32 * x32, axis=-1, keepdims=True) + EPS)


def _rope(x):
    s, d = x.shape[1], x.shape[-1]
    pos = jnp.arange(s, dtype=jnp.float32)
    inv_freq = ROPE_BASE ** (-jnp.arange(0, d, 2, dtype=jnp.float32) / d)
    ang = pos[:, None] * inv_freq[None, :]
    cos = jnp.cos(ang)[None, :, None, :]
    sin = jnp.sin(ang)[None, :, None, :]
    x32 = x.astype(jnp.float32)
    x1, x2 = x32[..., : d // 2], x32[..., d // 2:]
    return jnp.concatenate([x1 * cos - x2 * sin, x1 * sin + x2 * cos], axis=-1).astype(x.dtype)


def _forgetting_attention(q, k, v, log_f):
    b, s, h, d = q.shape
    scale = 1.0 / np.sqrt(d)
    cum = jnp.cumsum(log_f, axis=1).transpose(0, 2, 1)
    qh = q.transpose(0, 2, 1, 3)
    kh = k.transpose(0, 2, 1, 3)
    vh = v.transpose(0, 2, 1, 3)
    outs = []
    for blk in range(s // Q_BLOCK):
        s0, s1 = blk * Q_BLOCK, (blk + 1) * Q_BLOCK
        logits = jnp.einsum('bhqd,bhkd->bhqk', qh[:, :, s0:s1], kh[:, :, :s1]).astype(jnp.float32) * scale
        logits = logits + cum[:, :, s0:s1, None] - cum[:, :, None, :s1]
        q_pos = jnp.arange(s0, s1)[:, None]
        k_pos = jnp.arange(s1)[None, :]
        logits = jnp.where(q_pos >= k_pos, logits, -jnp.inf)
        p = jax.nn.softmax(logits, axis=-1).astype(v.dtype)
        outs.append(jnp.einsum('bhqk,bhkd->bhqd', p, vh[:, :, :s1]))
    out = jnp.concatenate(outs, axis=2)
    return out.transpose(0, 2, 1, 3)


def _retention(q, k, v):
    b, s, h, dk = q.shape
    dv = v.shape[-1]
    n_chunks = s // RET_CHUNK
    log_g = jnp.log(1.0 - 2.0 ** (-5.0 - jnp.arange(h, dtype=jnp.float32)))
    n = jnp.arange(RET_CHUNK, dtype=jnp.float32)
    diff = n[:, None] - n[None, :]
    decay_mask = jnp.where(diff[None] >= 0,
                           jnp.exp(jnp.maximum(diff, 0.0)[None] * log_g[:, None, None]), 0.0)
    xi = jnp.exp((n[None, :] + 1.0) * log_g[:, None])
    zeta = jnp.exp((RET_CHUNK - 1.0 - n[None, :]) * log_g[:, None])
    g_chunk = jnp.exp(RET_CHUNK * log_g)

    def to_chunks(t):
        return t.astype(jnp.float32).reshape(b, n_chunks, RET_CHUNK, h, t.shape[-1]).transpose(1, 0, 3, 2, 4)

    qc, kc, vc = to_chunks(q), to_chunks(k * (dk ** -0.5)), to_chunks(v)

    def step(state, inp):
        qi, ki, vi = inp
        inner = jnp.einsum('bhnd,bhmd->bhnm', qi, ki) * decay_mask[None]
        inner_out = jnp.einsum('bhnm,bhmv->bhnv', inner, vi)
        cross_out = jnp.einsum('bhnd,bhdv->bhnv', qi, state) * xi[None, :, :, None]
        new_state = state * g_chunk[None, :, None, None] + \
            jnp.einsum('bhmd,bhmv->bhdv', ki * zeta[None, :, :, None], vi)
        return new_state, inner_out + cross_out

    state0 = jnp.zeros((b, h, dk, dv), jnp.float32)
    _, out = lax.scan(step, state0, (qc, kc, vc))
    return out.transpose(1, 0, 3, 2, 4).reshape(b, s, h, dv)


def setup_inputs(seed: int = 0) -> dict:
    key = jax.random.key(seed)
    ks = jax.random.split(key, 16)
    nrm = jax.random.normal
    f32 = jnp.float32
    return {
        "x": nrm(ks[0], (BATCH, SEQ, D_MODEL), f32),
        "c": nrm(ks[1], (BATCH, D_MODEL), f32),
        "w_ada": nrm(ks[2], (DEPTH, D_MODEL, N_MOD * D_MODEL), f32) * (0.5 * D_MODEL ** -0.5),
        "b_ada": nrm(ks[3], (DEPTH, N_MOD * D_MODEL), f32) * 0.02,
        "w_in": nrm(ks[4], (DEPTH, D_MODEL, IN_COLS), f32) * D_MODEL ** -0.5,
        "b_forget": 2.0 + 0.5 * nrm(ks[5], (DEPTH, FOX_HEADS), f32),
        "q_norm_gain": 1.0 + 0.05 * nrm(ks[6], (DEPTH, FOX_HEAD_DIM), f32),
        "k_norm_gain": 1.0 + 0.05 * nrm(ks[7], (DEPTH, FOX_HEAD_DIM), f32),
        "fox_out_gain": 1.0 + 0.05 * nrm(ks[8], (DEPTH, FOX_HEADS, FOX_HEAD_DIM), f32),
        "ret_out_gain": 1.0 + 0.05 * nrm(ks[9], (DEPTH, RET_HEADS, RET_HEAD_DIM), f32),
        "w_out": nrm(ks[10], (DEPTH, MIX_WIDTH, D_MODEL), f32) * MIX_WIDTH ** -0.5,
        "w_mlp_in": nrm(ks[11], (DEPTH, D_MODEL, D_FF), f32) * D_MODEL ** -0.5,
        "w_mlp_out": nrm(ks[12], (DEPTH, D_FF, D_MODEL), f32) * D_FF ** -0.5,
    }


def reference(x, c, w_ada, b_ada, w_in, b_forget, q_norm_gain, k_norm_gain,
              fox_out_gain, ret_out_gain, w_out, w_mlp_in, w_mlp_out):
    b, s, _ = x.shape
    dt = x.dtype
    c_act = jax.nn.silu(c)
    o_fq, o_fk, o_fv, o_fog = 0, FOX_WIDTH, 2 * FOX_WIDTH, 3 * FOX_WIDTH
    o_ff = 4 * FOX_WIDTH
    o_rq = o_ff + FOX_HEADS
    o_rk, o_rv, o_rg = o_rq + RET_WIDTH, o_rq + 2 * RET_WIDTH, o_rq + 3 * RET_WIDTH
    for l in range(DEPTH):
        mod = jnp.einsum('bd,de->be', c_act, w_ada[l]) + b_ada[l]
        shift_m, scale_m, gate_m, shift_f, scale_f, gate_f = [m[:, None, :] for m in jnp.split(mod, N_MOD, axis=-1)]

        h = (_rms(x) * (1.0 + scale_m) + shift_m).astype(dt)
        proj = jnp.einsum('bsd,de->bse', h, w_in[l])

        fq = proj[..., o_fq:o_fk].reshape(b, s, FOX_HEADS, FOX_HEAD_DIM)
        fk = proj[..., o_fk:o_fv].reshape(b, s, FOX_HEADS, FOX_HEAD_DIM)
        fv = proj[..., o_fv:o_fog].reshape(b, s, FOX_HEADS, FOX_HEAD_DIM)
        f_og = proj[..., o_fog:o_ff]
        f_logit = proj[..., o_ff:o_rq]
        fq = (_rms(fq) * q_norm_gain[l]).astype(dt)
        fk = (_rms(fk) * k_norm_gain[l]).astype(dt)
        log_f = jax.nn.log_sigmoid(f_logit.astype(jnp.float32) + b_forget[l])
        fox = _forgetting_attention(fq, fk, fv, log_f)
        fox = (_rms(fox) * fox_out_gain[l]).reshape(b, s, FOX_WIDTH) * jax.nn.sigmoid(f_og.astype(jnp.float32))

        rq = _rope(proj[..., o_rq:o_rk].reshape(b, s, RET_HEADS, RET_HEAD_DIM))
        rk = _rope(proj[..., o_rk:o_rv].reshape(b, s, RET_HEADS, RET_HEAD_DIM))
        rv = proj[..., o_rv:o_rg].reshape(b, s, RET_HEADS, RET_HEAD_DIM)
        r_gate = proj[..., o_rg:]
        ret = _retention(rq, rk, rv)
        ret = (_rms(ret) * ret_out_gain[l]).reshape(b, s, RET_WIDTH) * jax.nn.silu(r_gate.astype(jnp.float32))

        mixed = jnp.concatenate([fox, ret], axis=-1).astype(dt)
        x = x + gate_m * jnp.einsum('bse,ed->bsd', mixed, w_out[l])

        h = (_rms(x) * (1.0 + scale_f) + shift_f).astype(dt)
        u = jnp.square(jax.nn.relu(jnp.einsum('bsd,df->bsf', h, w_mlp_in[l])))
        x = x + gate_f * jnp.einsum('bsf,fd->bsd', u, w_mlp_out[l])
    return x
```

```python
import functools

import jax
import jax.numpy as jnp
from jax import lax
from jax.experimental import pallas as pl
from jax.experimental.pallas import tpu as pltpu

FOX_HEADS = 8
FOX_HEAD_DIM = 64
FOX_WIDTH = FOX_HEADS * FOX_HEAD_DIM
RET_HEADS = 4
RET_HEAD_DIM = 128
RET_WIDTH = RET_HEADS * RET_HEAD_DIM
RET_CHUNK = 128
ROPE_BASE = 10000.0
EPS = 1e-6
N_MOD = 6

LANES = 128
VMEM_LIMIT = 56 * 1024 * 1024
MASK_VALUE = -1e30

TOKEN_TILE = 512
ATTN_TILE = 256
FF_CHUNK = 1024

bf16 = jnp.bfloat16
f32 = jnp.float32


def _dot(a, b):
    return jnp.dot(a, b, preferred_element_type=f32)


def _dot_nt(a, b):
    return lax.dot_general(a, b, (((1,), (1,)), ((), ())), preferred_element_type=f32)


def _half_lane_mask():
    return lax.broadcasted_iota(jnp.int32, (1, LANES), 1) < FOX_HEAD_DIM


def _pair_sumsq(v, lo):
    sq = v * v
    s_lo = jnp.sum(jnp.where(lo, sq, 0.0), axis=-1, keepdims=True)
    s_hi = jnp.sum(jnp.where(lo, 0.0, sq), axis=-1, keepdims=True)
    return jnp.where(lo, s_lo, s_hi)


def _mod_kernel(c_ref, w_ref, b_ref, o_ref):
    c = c_ref[...]
    c_act = (c * jax.nn.sigmoid(c)).astype(bf16)
    o_ref[...] = _dot(c_act, w_ref[...].astype(bf16)) + b_ref[...]


def _modulation(c, w_ada, b_ada):
    b, d = c.shape
    n = w_ada.shape[1]
    return pl.pallas_call(
        _mod_kernel,
        grid=(n // d,),
        in_specs=[
            pl.BlockSpec((b, d), lambda j: (0, 0)),
            pl.BlockSpec((d, d), lambda j: (0, j)),
            pl.BlockSpec((1, d), lambda j: (0, j)),
        ],
        out_specs=pl.BlockSpec((b, d), lambda j: (0, j)),
        out_shape=jax.ShapeDtypeStruct((b, n), f32),
        compiler_params=pltpu.CompilerParams(
            dimension_semantics=("arbitrary",), vmem_limit_bytes=VMEM_LIMIT),
        name="adaln_mod",
    )(c, w_ada, b_ada.reshape(1, n))


def _inproj_kernel(x_ref, mod_ref, wf_ref, wg_ref, wr_ref, gq_ref, gk_ref, bf_ref, cos_ref, sin_ref,
                   fq_ref, fk_ref, fv_ref, fog_ref, lf_ref, rq_ref, rk_ref, rv_ref, rg_ref):
    x = x_ref[0]
    inv = lax.rsqrt(jnp.mean(x * x, axis=-1, keepdims=True) + EPS)
    h = (x * inv * (1.0 + mod_ref[0, 1:2, :]) + mod_ref[0, 0:1, :]).astype(bf16)
    lo = _half_lane_mask()
    w = FOX_WIDTH

    def qk_norm(p, gain_ref):
        outs = []
        for g in range(w // LANES):
            v = p[:, g * LANES:(g + 1) * LANES]
            ss = _pair_sumsq(v, lo)
            outs.append(v * lax.rsqrt(ss * (1.0 / FOX_HEAD_DIM) + EPS))
        return (jnp.concatenate(outs, axis=-1) * gain_ref[...]).astype(bf16)

    fq_ref[0] = qk_norm(_dot(h, wf_ref[:, 0:w]), gq_ref)
    fk_ref[0] = qk_norm(_dot(h, wf_ref[:, w:2 * w]), gk_ref)
    fv_ref[0] = _dot(h, wf_ref[:, 2 * w:3 * w]).astype(bf16)
    fog_ref[0] = jax.nn.sigmoid(_dot(h, wf_ref[:, 3 * w:4 * w])).astype(bf16)

    z = _dot(h, wg_ref[...]) + bf_ref[...]
    log_f = jnp.minimum(z, 0.0) - jnp.log(1.0 + jnp.exp(-jnp.abs(z)))
    lf_ref[0] = log_f.T[0:FOX_HEADS, :]

    cos = cos_ref[...]
    sin = sin_ref[...]
    rw = RET_WIDTH

    def rope(p, scale):
        outs = []
        for g in range(RET_HEADS):
            v = p[:, g * LANES:(g + 1) * LANES]
            outs.append(v * cos + pltpu.roll(v, RET_HEAD_DIM // 2, 1) * sin)
        r = jnp.concatenate(outs, axis=-1)
        return (r * scale if scale != 1.0 else r).astype(bf16)

    rq_ref[0] = rope(_dot(h, wr_ref[:, 0:rw]), 1.0)
    rk_ref[0] = rope(_dot(h, wr_ref[:, rw:2 * rw]), RET_HEAD_DIM ** -0.5)
    rv_ref[0] = _dot(h, wr_ref[:, 2 * rw:3 * rw]).astype(bf16)
    gate = _dot(h, wr_ref[:, 3 * rw:4 * rw])
    rg_ref[0] = (gate * jax.nn.sigmoid(gate)).astype(bf16)


def _input_projection(x, mod, w_in, b_forget, q_gain, k_gain):
    b, s, d = x.shape
    tm = min(TOKEN_TILE, s)
    o_ff = 4 * FOX_WIDTH
    o_r = o_ff + FOX_HEADS
    w_fox = w_in[:, :o_ff].astype(bf16)
    w_fg = jnp.pad(w_in[:, o_ff:o_r], ((0, 0), (0, LANES - FOX_HEADS))).astype(bf16)
    w_ret = w_in[:, o_r:].astype(bf16)
    bias_f = jnp.pad(b_forget, (0, LANES - FOX_HEADS)).reshape(1, LANES)
    gq = (jnp.tile(q_gain, FOX_HEADS) * (FOX_HEAD_DIM ** -0.5)).reshape(1, FOX_WIDTH)
    gk = jnp.tile(k_gain, FOX_HEADS).reshape(1, FOX_WIDTH)

    half = RET_HEAD_DIM // 2
    pos = jnp.arange(s, dtype=f32)
    inv_freq = ROPE_BASE ** (-jnp.arange(0, RET_HEAD_DIM, 2, dtype=f32) / RET_HEAD_DIM)
    ang = pos[:, None] * inv_freq[None, :]
    cos_t = jnp.concatenate([jnp.cos(ang), jnp.cos(ang)], axis=-1)
    sin_t = jnp.concatenate([-jnp.sin(ang), jnp.sin(ang)], axis=-1)
    assert cos_t.shape == (s, 2 * half)

    tok = lambda width: pl.BlockSpec((1, tm, width), lambda i, j: (i, j, 0))
    full = lambda a: pl.BlockSpec(a.shape, lambda i, j: (0,) * a.ndim)
    act = lambda width: jax.ShapeDtypeStruct((b, s, width), bf16)
    return pl.pallas_call(
        _inproj_kernel,
        grid=(b, s // tm),
        in_specs=[
            tok(d),
            pl.BlockSpec((1, N_MOD, d), lambda i, j: (i, 0, 0)),
            full(w_fox), full(w_fg), full(w_ret), full(gq), full(gk), full(bias_f),
            pl.BlockSpec((tm, LANES), lambda i, j: (j, 0)),
            pl.BlockSpec((tm, LANES), lambda i, j: (j, 0)),
        ],
        out_specs=[
            tok(FOX_WIDTH), tok(FOX_WIDTH), tok(FOX_WIDTH), tok(FOX_WIDTH),
            pl.BlockSpec((1, FOX_HEADS, tm), lambda i, j: (i, 0, j)),
            tok(RET_WIDTH), tok(RET_WIDTH), tok(RET_WIDTH), tok(RET_WIDTH),
        ],
        out_shape=[
            act(FOX_WIDTH), act(FOX_WIDTH), act(FOX_WIDTH), act(FOX_WIDTH),
            jax.ShapeDtypeStruct((b, FOX_HEADS, s), f32),
            act(RET_WIDTH), act(RET_WIDTH), act(RET_WIDTH), act(RET_WIDTH),
        ],
        compiler_params=pltpu.CompilerParams(
            dimension_semantics=("arbitrary", "arbitrary"), vmem_limit_bytes=VMEM_LIMIT),
        name="in_proj",
    )(x, mod, w_fox, w_fg, w_ret, gq, gk, bias_f, cos_t, sin_t)


def _lane_cumsum(x, out_ref):
    rows, s = x.shape
    r = lax.broadcasted_iota(jnp.int32, (LANES, LANES), 0)
    c = lax.broadcasted_iota(jnp.int32, (LANES, LANES), 1)
    tri = (r <= c).astype(bf16)
    hi = x.astype(bf16).astype(f32)
    rest = x - hi
    mid = rest.astype(bf16).astype(f32)
    low = (rest - mid).astype(bf16).astype(f32)
    offset = jnp.zeros((rows, 1), f32)
    for ch in range(s // LANES):
        sl = slice(ch * LANES, (ch + 1) * LANES)
        pieces = jnp.concatenate([hi[:, sl], mid[:, sl], low[:, sl], jnp.zeros((rows, LANES), f32)], axis=0)
        part = _dot(pieces.astype(bf16), tri)
        cum = part[0:rows] + part[rows:2 * rows] + part[2 * rows:3 * rows] + offset
        out_ref[:, sl] = cum
        offset = cum[:, LANES - 1:LANES]


def _fox_kernel(q_ref, k_ref, v_ref, lf_ref, og_ref, gain_ref, o_ref, cum_ref, acc_ref, m_ref, l_ref):
    pair = pl.program_id(1)
    s = q_ref.shape[1]
    t = min(ATTN_TILE, s)

    @pl.when(pair == 0)
    def _():
        _lane_cumsum(lf_ref[0], cum_ref)

    lo = _half_lane_mask()
    head_masks = (lo, jnp.logical_not(lo))
    row = lax.broadcasted_iota(jnp.int32, (t, t), 0)
    col = lax.broadcasted_iota(jnp.int32, (t, t), 1)
    causal = row >= col

    def q_tile(qi, _):
        q0 = pl.multiple_of(qi * t, t)
        q = q_ref[0, pl.ds(q0, t), :]
        qm = [jnp.where(hm, q, jnp.zeros_like(q)) for hm in head_masks]
        m_ref[...] = jnp.full(m_ref.shape, MASK_VALUE, f32)
        l_ref[...] = jnp.zeros(l_ref.shape, f32)
        acc_ref[...] = jnp.zeros(acc_ref.shape, f32)

        def k_block(kb, masked):
            k0 = pl.multiple_of(kb * t, t)
            k = k_ref[0, pl.ds(k0, t), :]
            v = v_ref[0, pl.ds(k0, t), :]
            for hh in range(2):
                bias = cum_ref[pl.ds(2 * pair + hh, 1), pl.ds(k0, t)]
                sc = _dot_nt(qm[hh], k) - bias
                if masked:
                    sc = jnp.where(causal, sc, MASK_VALUE)
                m_prev = m_ref[hh]
                m_new = jnp.maximum(m_prev, jnp.max(sc, axis=-1, keepdims=True))
                alpha = jnp.exp(m_prev - m_new)
                p = jnp.exp(sc - m_new[:, 0:1])
                l_ref[hh] = alpha * l_ref[hh] + jnp.sum(p, axis=-1, keepdims=True)
                acc_ref[hh] = alpha * acc_ref[hh] + _dot(p.astype(bf16), v)
                m_ref[hh] = m_new

        def full_block(kb, c):
            k_block(kb, False)
            return c

        lax.fori_loop(0, qi, full_block, 0)
        k_block(qi, True)

        out = jnp.where(lo, acc_ref[0] / l_ref[0], acc_ref[1] / l_ref[1])
        ss = _pair_sumsq(out, lo)
        out = out * lax.rsqrt(ss * (1.0 / FOX_HEAD_DIM) + EPS) * gain_ref[0]
        o_ref[0, pl.ds(q0, t), :] = (out * og_ref[0, pl.ds(q0, t), :].astype(f32)).astype(bf16)
        return _

    lax.fori_loop(0, s // t, q_tile, 0)


def _fox_attention(fq, fk, fv, fog, log_f, out_gain):
    b, s, _ = fq.shape
    t = min(ATTN_TILE, s)
    pairs = FOX_WIDTH // LANES
    gain = out_gain.reshape(pairs, 1, LANES)
    seq = pl.BlockSpec((1, s, LANES), lambda i, p: (i, 0, p))
    return pl.pallas_call(
        _fox_kernel,
        grid=(b, pairs),
        in_specs=[
            seq, seq, seq,
            pl.BlockSpec((1, FOX_HEADS, s), lambda i, p: (i, 0, 0)),
            seq,
            pl.BlockSpec((1, 1, LANES), lambda i, p: (p, 0, 0)),
        ],
        out_specs=seq,
        out_shape=jax.ShapeDtypeStruct((b, s, FOX_WIDTH), bf16),
        scratch_shapes=[
            pltpu.VMEM((FOX_HEADS, s), f32),
            pltpu.VMEM((2, t, LANES), f32),
            pltpu.VMEM((2, t, LANES), f32),
            pltpu.VMEM((2, t, LANES), f32),
        ],
        compiler_params=pltpu.CompilerParams(
            dimension_semantics=("arbitrary", "arbitrary"), vmem_limit_bytes=VMEM_LIMIT),
        name="fox_attention",
    )(fq, fk, fv, log_f, fog, gain)


def _ret_kernel(q_ref, k_ref, v_ref, g_ref, cst_ref, gain_ref, o_ref, st_ref):
    s = q_ref.shape[1]
    c = RET_CHUNK
    st_ref[...] = jnp.zeros(st_ref.shape, f32)

    def chunk(ci, carry):
        r0 = pl.multiple_of(ci * c, c)
        for hd in range(RET_HEADS):
            cs = slice(hd * RET_HEAD_DIM, (hd + 1) * RET_HEAD_DIM)
            q = q_ref[0, pl.ds(r0, c), cs]
            k = k_ref[0, pl.ds(r0, c), cs]
            v = v_ref[0, pl.ds(r0, c), cs]
            inner = _dot_nt(q, k) * cst_ref[hd, 0]
            out = _dot(inner.astype(bf16), v)
            state = st_ref[hd]
            out = out + _dot(q, state.astype(bf16)) * cst_ref[hd, 1]
            kz = (k.astype(f32) * cst_ref[hd, 2]).T.astype(bf16)
            st_ref[hd] = state * cst_ref[hd, 3] + _dot(kz, v)
            inv = lax.rsqrt(jnp.mean(out * out, axis=-1, keepdims=True) + EPS)
            gate = g_ref[0, pl.ds(r0, c), cs].astype(f32)
            o_ref[0, pl.ds(r0, c), cs] = (out * inv * gain_ref[:, cs] * gate).astype(bf16)
        return carry

    lax.fori_loop(0, s // c, chunk, 0)


def _retention_constants():
    c = RET_CHUNK
    log_g = jnp.log(1.0 - 2.0 ** (-5.0 - jnp.arange(RET_HEADS, dtype=f32)))
    n = jnp.arange(c, dtype=f32)
    diff = n[:, None] - n[None, :]
    mask = jnp.where(diff[None] >= 0, jnp.exp(jnp.maximum(diff, 0.0)[None] * log_g[:, None, None]), 0.0)
    xi = jnp.exp((n[None, :] + 1.0) * log_g[:, None])
    zeta = jnp.exp((c - 1.0 - n[None, :]) * log_g[:, None])
    g_chunk = jnp.exp(c * log_g)
    bc = lambda rows: jnp.broadcast_to(rows[:, :, None], (RET_HEADS, c, c))
    return jnp.stack([mask, bc(xi), bc(zeta), jnp.broadcast_to(g_chunk[:, None, None], (RET_HEADS, c, c))], axis=1)


def _retention(rq, rk, rv, rg, out_gain):
    b, s, _ = rq.shape
    cst = _retention_constants()
    gain = out_gain.reshape(1, RET_WIDTH)
    seq = pl.BlockSpec((1, s, RET_WIDTH), lambda i: (i, 0, 0))
    return pl.pallas_call(
        _ret_kernel,
        grid=(b,),
        in_specs=[seq, seq, seq, seq,
                  pl.BlockSpec(cst.shape, lambda i: (0, 0, 0, 0)),
                  pl.BlockSpec(gain.shape, lambda i: (0, 0))],
        out_specs=seq,
        out_shape=jax.ShapeDtypeStruct((b, s, RET_WIDTH), bf16),
        scratch_shapes=[pltpu.VMEM((RET_HEADS, RET_HEAD_DIM, RET_HEAD_DIM), f32)],
        compiler_params=pltpu.CompilerParams(
            dimension_semantics=("arbitrary",), vmem_limit_bytes=VMEM_LIMIT),
        name="retention",
    )(rq, rk, rv, rg, cst, gain)


def _out_mlp_kernel(x_ref, mf_ref, mr_ref, mod_ref, wo_ref, w1_ref, w2_ref, o_ref):
    gate_m = mod_ref[0, 2:3, :]
    shift_f = mod_ref[0, 3:4, :]
    scale_f = mod_ref[0, 4:5, :]
    gate_f = mod_ref[0, 5:6, :]
    mixed = _dot(mf_ref[0], wo_ref[0:FOX_WIDTH, :]) + _dot(mr_ref[0], wo_ref[FOX_WIDTH:, :])
    x1 = x_ref[0] + gate_m * mixed
    inv = lax.rsqrt(jnp.mean(x1 * x1, axis=-1, keepdims=True) + EPS)
    h = (x1 * inv * (1.0 + scale_f) + shift_f).astype(bf16)
    d_ff = w1_ref.shape[1]
    fc = min(FF_CHUNK, d_ff)
    y = jnp.zeros(x1.shape, f32)
    for j in range(d_ff // fc):
        u = jnp.maximum(_dot(h, w1_ref[:, j * fc:(j + 1) * fc]), 0.0)
        y = y + _dot((u * u).astype(bf16), w2_ref[j * fc:(j + 1) * fc, :])
    o_ref[0] = x1 + gate_f * y


def _out_mlp(x, mixed_fox, mixed_ret, mod, w_out, w_mlp_in, w_mlp_out):
    b, s, d = x.shape
    tm = min(TOKEN_TILE, s)
    wo = w_out.astype(bf16)
    w1 = w_mlp_in.astype(bf16)
    w2 = w_mlp_out.astype(bf16)
    tok = lambda width: pl.BlockSpec((1, tm, width), lambda i, j: (i, j, 0))
    resident = lambda a: pl.BlockSpec(a.shape, lambda i, j: (0, 0), pipeline_mode=pl.Buffered(1))
    return pl.pallas_call(
        _out_mlp_kernel,
        grid=(b, s // tm),
        in_specs=[
            tok(d), tok(FOX_WIDTH), tok(RET_WIDTH),
            pl.BlockSpec((1, N_MOD, d), lambda i, j: (i, 0, 0)),
            resident(wo), resident(w1), resident(w2),
        ],
        out_specs=tok(d),
        out_shape=jax.ShapeDtypeStruct((b, s, d), f32),
        compiler_params=pltpu.CompilerParams(
            dimension_semantics=("arbitrary", "arbitrary"), vmem_limit_bytes=VMEM_LIMIT),
        name="out_mlp",
    )(x, mixed_fox, mixed_ret, mod, wo, w1, w2)


def kernel(x, c, w_ada, b_ada, w_in, b_forget, q_norm_gain, k_norm_gain, fox_out_gain, ret_out_gain,
           w_out, w_mlp_in, w_mlp_out):
    b, s, d = x.shape
    for l in range(w_ada.shape[0]):
        mod = _modulation(c, w_ada[l], b_ada[l]).reshape(b, N_MOD, d)
        fq, fk, fv, fog, log_f, rq, rk, rv, rg = _input_projection(
            x, mod, w_in[l], b_forget[l], q_norm_gain[l], k_norm_gain[l])
        mixed_fox = _fox_attention(fq, fk, fv, fog, log_f, fox_out_gain[l])
        mixed_ret = _retention(rq, rk, rv, rg, ret_out_gain[l])
        x = _out_mlp(x, mixed_fox, mixed_ret, mod, w_out[l], w_mlp_in[l], w_mlp_out[l])
    return x
```

```python
import functools

import jax
import jax.numpy as jnp
from jax import lax
from jax.experimental import pallas as pl
from jax.experimental.pallas import tpu as pltpu

FOX_HEADS = 8
FOX_HEAD_DIM = 64
FOX_WIDTH = FOX_HEADS * FOX_HEAD_DIM
RET_HEADS = 4
RET_HEAD_DIM = 128
RET_WIDTH = RET_HEADS * RET_HEAD_DIM
RET_CHUNK = 128
ROPE_BASE = 10000.0
EPS = 1e-6
N_MOD = 6

LANES = 128
VMEM_LIMIT = 56 * 1024 * 1024
MASK_VALUE = -1e30

TOKEN_TILE = 512
ATTN_TILE = 256
FF_CHUNK = 1024

bf16 = jnp.bfloat16
f32 = jnp.float32


def _dot(a, b):
    return jnp.dot(a, b, preferred_element_type=f32)


def _dot_nt(a, b):
    return lax.dot_general(a, b, (((1,), (1,)), ((), ())), preferred_element_type=f32)


def _half_lane_mask():
    return lax.broadcasted_iota(jnp.int32, (1, LANES), 1) < FOX_HEAD_DIM


def _pair_sumsq(v, lo):
    sq = v * v
    s_lo = jnp.sum(jnp.where(lo, sq, 0.0), axis=-1, keepdims=True)
    s_hi = jnp.sum(jnp.where(lo, 0.0, sq), axis=-1, keepdims=True)
    return jnp.where(lo, s_lo, s_hi)


def _mod_kernel(c_ref, w_ref, b_ref, o_ref):
    c = c_ref[...]
    c_act = (c * jax.nn.sigmoid(c)).astype(bf16)
    o_ref[...] = _dot(c_act, w_ref[...].astype(bf16)) + b_ref[...]


def _modulation(c, w_ada, b_ada):
    b, d = c.shape
    n = w_ada.shape[1]
    return pl.pallas_call(
        _mod_kernel,
        grid=(n // d,),
        in_specs=[
            pl.BlockSpec((b, d), lambda j: (0, 0)),
            pl.BlockSpec((d, d), lambda j: (0, j)),
            pl.BlockSpec((1, d), lambda j: (0, j)),
        ],
        out_specs=pl.BlockSpec((b, d), lambda j: (0, j)),
        out_shape=jax.ShapeDtypeStruct((b, n), f32),
        compiler_params=pltpu.CompilerParams(
            dimension_semantics=("arbitrary",), vmem_limit_bytes=VMEM_LIMIT),
        name="adaln_mod",
    )(c, w_ada, b_ada.reshape(1, n))


def _inproj_kernel(x_ref, mod_ref, wf_ref, wg_ref, wr_ref, gq_ref, gk_ref, bf_ref, cos_ref, sin_ref,
                   fq_ref, fk_ref, fv_ref, fog_ref, lf_ref, rq_ref, rk_ref, rv_ref, rg_ref):
    x = x_ref[0]
    inv = lax.rsqrt(jnp.mean(x * x, axis=-1, keepdims=True) + EPS)
    h = (x * inv * (1.0 + mod_ref[0, 1:2, :]) + mod_ref[0, 0:1, :]).astype(bf16)
    lo = _half_lane_mask()
    w = FOX_WIDTH

    def qk_norm(p, gain_ref):
        outs = []
        for g in range(w // LANES):
            v = p[:, g * LANES:(g + 1) * LANES]
            ss = _pair_sumsq(v, lo)
            outs.append(v * lax.rsqrt(ss * (1.0 / FOX_HEAD_DIM) + EPS))
        return (jnp.concatenate(outs, axis=-1) * gain_ref[...]).astype(bf16)

    fq_ref[0] = qk_norm(_dot(h, wf_ref[:, 0:w]), gq_ref)
    fk_ref[0] = qk_norm(_dot(h, wf_ref[:, w:2 * w]), gk_ref)
    fv_ref[0] = _dot(h, wf_ref[:, 2 * w:3 * w]).astype(bf16)
    fog_ref[0] = jax.nn.sigmoid(_dot(h, wf_ref[:, 3 * w:4 * w])).astype(bf16)

    z = _dot(h, wg_ref[...]) + bf_ref[...]
    log_f = jnp.minimum(z, 0.0) - jnp.log(1.0 + jnp.exp(-jnp.abs(z)))
    lf_ref[0] = log_f.T[0:FOX_HEADS, :]

    cos = cos_ref[...]
    sin = sin_ref[...]
    rw = RET_WIDTH

    def rope(p, scale):
        outs = []
        for g in range(RET_HEADS):
            v = p[:, g * LANES:(g + 1) * LANES]
            outs.append(v * cos + pltpu.roll(v, RET_HEAD_DIM // 2, 1) * sin)
        r = jnp.concatenate(outs, axis=-1)
        return (r * scale if scale != 1.0 else r).astype(bf16)

    rq_ref[0] = rope(_dot(h, wr_ref[:, 0:rw]), 1.0)
    rk_ref[0] = rope(_dot(h, wr_ref[:, rw:2 * rw]), RET_HEAD_DIM ** -0.5)
    rv_ref[0] = _dot(h, wr_ref[:, 2 * rw:3 * rw]).astype(bf16)
    gate = _dot(h, wr_ref[:, 3 * rw:4 * rw])
    rg_ref[0] = (gate * jax.nn.sigmoid(gate)).astype(bf16)


def _input_projection(x, mod, w_in, b_forget, q_gain, k_gain):
    b, s, d = x.shape
    tm = min(TOKEN_TILE, s)
    o_ff = 4 * FOX_WIDTH
    o_r = o_ff + FOX_HEADS
    w_fox = w_in[:, :o_ff].astype(bf16)
    w_fg = jnp.pad(w_in[:, o_ff:o_r], ((0, 0), (0, LANES - FOX_HEADS))).astype(bf16)
    w_ret = w_in[:, o_r:].astype(bf16)
    bias_f = jnp.pad(b_forget, (0, LANES - FOX_HEADS)).reshape(1, LANES)
    gq = (jnp.tile(q_gain, FOX_HEADS) * (FOX_HEAD_DIM ** -0.5)).reshape(1, FOX_WIDTH)
    gk = jnp.tile(k_gain, FOX_HEADS).reshape(1, FOX_WIDTH)

    half = RET_HEAD_DIM // 2
    pos = jnp.arange(s, dtype=f32)
    inv_freq = ROPE_BASE ** (-jnp.arange(0, RET_HEAD_DIM, 2, dtype=f32) / RET_HEAD_DIM)
    ang = pos[:, None] * inv_freq[None, :]
    cos_t = jnp.concatenate([jnp.cos(ang), jnp.cos(ang)], axis=-1)
    sin_t = jnp.concatenate([-jnp.sin(ang), jnp.sin(ang)], axis=-1)
    assert cos_t.shape == (s, 2 * half)

    tok = lambda width: pl.BlockSpec((1, tm, width), lambda i, j: (i, j, 0))
    full = lambda a: pl.BlockSpec(a.shape, lambda i, j: (0,) * a.ndim)
    act = lambda width: jax.ShapeDtypeStruct((b, s, width), bf16)
    return pl.pallas_call(
        _inproj_kernel,
        grid=(b, s // tm),
        in_specs=[
            tok(d),
            pl.BlockSpec((1, N_MOD, d), lambda i, j: (i, 0, 0)),
            full(w_fox), full(w_fg), full(w_ret), full(gq), full(gk), full(bias_f),
            pl.BlockSpec((tm, LANES), lambda i, j: (j, 0)),
            pl.BlockSpec((tm, LANES), lambda i, j: (j, 0)),
        ],
        out_specs=[
            tok(FOX_WIDTH), tok(FOX_WIDTH), tok(FOX_WIDTH), tok(FOX_WIDTH),
            pl.BlockSpec((1, FOX_HEADS, tm), lambda i, j: (i, 0, j)),
            tok(RET_WIDTH), tok(RET_WIDTH), tok(RET_WIDTH), tok(RET_WIDTH),
        ],
        out_shape=[
            act(FOX_WIDTH), act(FOX_WIDTH), act(FOX_WIDTH), act(FOX_WIDTH),
            jax.ShapeDtypeStruct((b, FOX_HEADS, s), f32),
            act(RET_WIDTH), act(RET_WIDTH), act(RET_WIDTH), act(RET_WIDTH),
        ],
        compiler_params=pltpu.CompilerParams(
            dimension_semantics=("arbitrary", "arbitrary"), vmem_limit_bytes=VMEM_LIMIT),
        name="in_proj",
    )(x, mod, w_fox, w_fg, w_ret, gq, gk, bias_f, cos_t, sin_t)


def _lane_cumsum(x, out_ref):
    rows, s = x.shape
    r = lax.broadcasted_iota(jnp.int32, (LANES, LANES), 0)
    c = lax.broadcasted_iota(jnp.int32, (LANES, LANES), 1)
    tri = (r <= c).astype(bf16)
    hi = x.astype(bf16).astype(f32)
    rest = x - hi
    mid = rest.astype(bf16).astype(f32)
    low = (rest - mid).astype(bf16).astype(f32)
    offset = jnp.zeros((rows, 1), f32)
    for ch in range(s // LANES):
        sl = slice(ch * LANES, (ch + 1) * LANES)
        pieces = jnp.concatenate([hi[:, sl], mid[:, sl], low[:, sl], jnp.zeros((rows, LANES), f32)], axis=0)
        part = _dot(pieces.astype(bf16), tri)
        cum = part[0:rows] + part[rows:2 * rows] + part[2 * rows:3 * rows] + offset
        out_ref[:, sl] = cum
        offset = cum[:, LANES - 1:LANES]


def _fox_kernel(q_ref, k_ref, v_ref, lf_ref, og_ref, gain_ref, o_ref, cum_ref, s_ref, p_ref):
    pair = pl.program_id(1)
    s = q_ref.shape[1]
    t = min(ATTN_TILE, s)

    @pl.when(pair == 0)
    def _():
        _lane_cumsum(lf_ref[0], cum_ref)

    lo = _half_lane_mask()
    row = lax.broadcasted_iota(jnp.int32, (t, t), 0)
    col = lax.broadcasted_iota(jnp.int32, (t, t), 1)
    causal = row >= col
    cum_a = cum_ref[pl.ds(2 * pair, 1), :]
    cum_b = cum_ref[pl.ds(2 * pair + 1, 1), :]

    for qi in range(s // t):
        rows = slice(qi * t, (qi + 1) * t)
        kext = (qi + 1) * t
        q = q_ref[0, rows, :]
        zero = jnp.zeros_like(q)
        q2 = jnp.concatenate([jnp.where(lo, q, zero), jnp.where(lo, zero, q)], axis=0)

        m_run = jnp.full((2 * t, LANES), MASK_VALUE, f32)
        for kb in range(qi + 1):
            ks = slice(kb * t, (kb + 1) * t)
            sc = _dot_nt(q2, k_ref[0, ks, :])
            sa = sc[0:t] - cum_a[:, ks]
            sb = sc[t:] - cum_b[:, ks]
            if kb == qi:
                sa = jnp.where(causal, sa, MASK_VALUE)
                sb = jnp.where(causal, sb, MASK_VALUE)
            sc = jnp.concatenate([sa, sb], axis=0)
            s_ref[:, ks] = sc
            for j in range(t // LANES):
                m_run = jnp.maximum(m_run, sc[:, j * LANES:(j + 1) * LANES])
        m = jnp.max(m_run, axis=-1, keepdims=True)

        l_run = jnp.zeros((2 * t, LANES), f32)
        for kb in range(qi + 1):
            ks = slice(kb * t, (kb + 1) * t)
            p = jnp.exp(s_ref[:, ks] - m)
            p_ref[:, ks] = p.astype(bf16)
            for j in range(t // LANES):
                l_run = l_run + p[:, j * LANES:(j + 1) * LANES]
        l = jnp.sum(l_run, axis=-1, keepdims=True)

        o2 = _dot(p_ref[:, 0:kext], v_ref[0, 0:kext, :]) / l
        out = jnp.where(lo, o2[0:t], o2[t:])
        ss = _pair_sumsq(out, lo)
        out = out * lax.rsqrt(ss * (1.0 / FOX_HEAD_DIM) + EPS) * gain_ref[0]
        o_ref[0, rows, :] = (out * og_ref[0, rows, :].astype(f32)).astype(bf16)


def _fox_attention(fq, fk, fv, fog, log_f, out_gain):
    b, s, _ = fq.shape
    t = min(ATTN_TILE, s)
    pairs = FOX_WIDTH // LANES
    gain = out_gain.reshape(pairs, 1, LANES)
    seq = pl.BlockSpec((1, s, LANES), lambda i, p: (i, 0, p))
    return pl.pallas_call(
        _fox_kernel,
        grid=(b, pairs),
        in_specs=[
            seq, seq, seq,
            pl.BlockSpec((1, FOX_HEADS, s), lambda i, p: (i, 0, 0)),
            seq,
            pl.BlockSpec((1, 1, LANES), lambda i, p: (p, 0, 0)),
        ],
        out_specs=seq,
        out_shape=jax.ShapeDtypeStruct((b, s, FOX_WIDTH), bf16),
        scratch_shapes=[
            pltpu.VMEM((FOX_HEADS, s), f32),
            pltpu.VMEM((2 * t, s), f32),
            pltpu.VMEM((2 * t, s), bf16),
        ],
        compiler_params=pltpu.CompilerParams(
            dimension_semantics=("arbitrary", "arbitrary"), vmem_limit_bytes=VMEM_LIMIT),
        name="fox_attention",
    )(fq, fk, fv, log_f, fog, gain)


def _ret_kernel(q_ref, k_ref, v_ref, g_ref, cst_ref, gain_ref, o_ref, st_ref):
    s = q_ref.shape[1]
    c = RET_CHUNK
    st_ref[...] = jnp.zeros(st_ref.shape, f32)

    def chunk(ci, carry):
        r0 = pl.multiple_of(ci * c, c)
        for hd in range(RET_HEADS):
            cs = slice(hd * RET_HEAD_DIM, (hd + 1) * RET_HEAD_DIM)
            q = q_ref[0, pl.ds(r0, c), cs]
            k = k_ref[0, pl.ds(r0, c), cs]
            v = v_ref[0, pl.ds(r0, c), cs]
            inner = _dot_nt(q, k) * cst_ref[hd, 0]
            out = _dot(inner.astype(bf16), v)
            state = st_ref[hd]
            out = out + _dot(q, state.astype(bf16)) * cst_ref[hd, 1]
            kz = (k.astype(f32) * cst_ref[hd, 2]).T.astype(bf16)
            st_ref[hd] = state * cst_ref[hd, 3] + _dot(kz, v)
            inv = lax.rsqrt(jnp.mean(out * out, axis=-1, keepdims=True) + EPS)
            gate = g_ref[0, pl.ds(r0, c), cs].astype(f32)
            o_ref[0, pl.ds(r0, c), cs] = (out * inv * gain_ref[:, cs] * gate).astype(bf16)
        return carry

    lax.fori_loop(0, s // c, chunk, 0)


def _retention_constants():
    c = RET_CHUNK
    log_g = jnp.log(1.0 - 2.0 ** (-5.0 - jnp.arange(RET_HEADS, dtype=f32)))
    n = jnp.arange(c, dtype=f32)
    diff = n[:, None] - n[None, :]
    mask = jnp.where(diff[None] >= 0, jnp.exp(jnp.maximum(diff, 0.0)[None] * log_g[:, None, None]), 0.0)
    xi = jnp.exp((n[None, :] + 1.0) * log_g[:, None])
    zeta = jnp.exp((c - 1.0 - n[None, :]) * log_g[:, None])
    g_chunk = jnp.exp(c * log_g)
    bc = lambda rows: jnp.broadcast_to(rows[:, :, None], (RET_HEADS, c, c))
    return jnp.stack([mask, bc(xi), bc(zeta), jnp.broadcast_to(g_chunk[:, None, None], (RET_HEADS, c, c))], axis=1)


def _retention(rq, rk, rv, rg, out_gain):
    b, s, _ = rq.shape
    cst = _retention_constants()
    gain = out_gain.reshape(1, RET_WIDTH)
    seq = pl.BlockSpec((1, s, RET_WIDTH), lambda i: (i, 0, 0))
    return pl.pallas_call(
        _ret_kernel,
        grid=(b,),
        in_specs=[seq, seq, seq, seq,
                  pl.BlockSpec(cst.shape, lambda i: (0, 0, 0, 0)),
                  pl.BlockSpec(gain.shape, lambda i: (0, 0))],
        out_specs=seq,
        out_shape=jax.ShapeDtypeStruct((b, s, RET_WIDTH), bf16),
        scratch_shapes=[pltpu.VMEM((RET_HEADS, RET_HEAD_DIM, RET_HEAD_DIM), f32)],
        compiler_params=pltpu.CompilerParams(
            dimension_semantics=("arbitrary",), vmem_limit_bytes=VMEM_LIMIT),
        name="retention",
    )(rq, rk, rv, rg, cst, gain)


def _out_mlp_kernel(x_ref, mf_ref, mr_ref, mod_ref, wo_ref, w1_ref, w2_ref, o_ref):
    gate_m = mod_ref[0, 2:3, :]
    shift_f = mod_ref[0, 3:4, :]
    scale_f = mod_ref[0, 4:5, :]
    gate_f = mod_ref[0, 5:6, :]
    mixed = _dot(mf_ref[0], wo_ref[0:FOX_WIDTH, :]) + _dot(mr_ref[0], wo_ref[FOX_WIDTH:, :])
    x1 = x_ref[0] + gate_m * mixed
    inv = lax.rsqrt(jnp.mean(x1 * x1, axis=-1, keepdims=True) + EPS)
    h = (x1 * inv * (1.0 + scale_f) + shift_f).astype(bf16)
    d_ff = w1_ref.shape[1]
    fc = min(FF_CHUNK, d_ff)
    y = jnp.zeros(x1.shape, f32)
    for j in range(d_ff // fc):
        u = jnp.maximum(_dot(h, w1_ref[:, j * fc:(j + 1) * fc]), 0.0)
        y = y + _dot((u * u).astype(bf16), w2_ref[j * fc:(j + 1) * fc, :])
    o_ref[0] = x1 + gate_f * y


def _out_mlp(x, mixed_fox, mixed_ret, mod, w_out, w_mlp_in, w_mlp_out):
    b, s, d = x.shape
    tm = min(TOKEN_TILE, s)
    wo = w_out.astype(bf16)
    w1 = w_mlp_in.astype(bf16)
    w2 = w_mlp_out.astype(bf16)
    tok = lambda width: pl.BlockSpec((1, tm, width), lambda i, j: (i, j, 0))
    resident = lambda a: pl.BlockSpec(a.shape, lambda i, j: (0, 0), pipeline_mode=pl.Buffered(1))
    return pl.pallas_call(
        _out_mlp_kernel,
        grid=(b, s // tm),
        in_specs=[
            tok(d), tok(FOX_WIDTH), tok(RET_WIDTH),
            pl.BlockSpec((1, N_MOD, d), lambda i, j: (i, 0, 0)),
            resident(wo), resident(w1), resident(w2),
        ],
        out_specs=tok(d),
        out_shape=jax.ShapeDtypeStruct((b, s, d), f32),
        compiler_params=pltpu.CompilerParams(
            dimension_semantics=("arbitrary", "arbitrary"), vmem_limit_bytes=VMEM_LIMIT),
        name="out_mlp",
    )(x, mixed_fox, mixed_ret, mod, wo, w1, w2)


def kernel(x, c, w_ada, b_ada, w_in, b_forget, q_norm_gain, k_norm_gain, fox_out_gain, ret_out_gain,
           w_out, w_mlp_in, w_mlp_out):
    b, s, d = x.shape
    for l in range(w_ada.shape[0]):
        mod = _modulation(c, w_ada[l], b_ada[l]).reshape(b, N_MOD, d)
        fq, fk, fv, fog, log_f, rq, rk, rv, rg = _input_projection(
            x, mod, w_in[l], b_forget[l], q_norm_gain[l], k_norm_gain[l])
        mixed_fox = _fox_attention(fq, fk, fv, fog, log_f, fox_out_gain[l])
        mixed_ret = _retention(rq, rk, rv, rg, ret_out_gain[l])
        x = _out_mlp(x, mixed_fox, mixed_ret, mod, w_out[l], w_mlp_in[l], w_mlp_out[l])
    return x
```

```python
import functools

import jax
import jax.numpy as jnp
from jax import lax
from jax.experimental import pallas as pl
from jax.experimental.pallas import tpu as pltpu

FOX_HEADS = 8
FOX_HEAD_DIM = 64
FOX_WIDTH = FOX_HEADS * FOX_HEAD_DIM
RET_HEADS = 4
RET_HEAD_DIM = 128
RET_WIDTH = RET_HEADS * RET_HEAD_DIM
RET_CHUNK = 128
ROPE_BASE = 10000.0
EPS = 1e-6
N_MOD = 6

LANES = 128
VMEM_LIMIT = 56 * 1024 * 1024
MASK_VALUE = -1e30
LOG2E = 1.4426950408889634

TOKEN_TILE = 512
ATTN_TILE = 256
FF_CHUNK = 1024

bf16 = jnp.bfloat16
f32 = jnp.float32


def _dot(a, b):
    return jnp.dot(a, b, preferred_element_type=f32)


def _dot_nt(a, b):
    return lax.dot_general(a, b, (((1,), (1,)), ((), ())), preferred_element_type=f32)


def _half_lane_mask():
    return lax.broadcasted_iota(jnp.int32, (1, LANES), 1) < FOX_HEAD_DIM


def _pair_sumsq(v, lo):
    sq = v * v
    s_lo = jnp.sum(jnp.where(lo, sq, 0.0), axis=-1, keepdims=True)
    s_hi = jnp.sum(jnp.where(lo, 0.0, sq), axis=-1, keepdims=True)
    return jnp.where(lo, s_lo, s_hi)


def _mod_kernel(c_ref, w_ref, b_ref, o_ref):
    c = c_ref[...]
    c_act = (c * jax.nn.sigmoid(c)).astype(bf16)
    o_ref[...] = _dot(c_act, w_ref[...].astype(bf16)) + b_ref[...]


def _modulation(c, w_ada, b_ada):
    b, d = c.shape
    n = w_ada.shape[1]
    return pl.pallas_call(
        _mod_kernel,
        grid=(n // d,),
        in_specs=[
            pl.BlockSpec((b, d), lambda j: (0, 0)),
            pl.BlockSpec((d, d), lambda j: (0, j)),
            pl.BlockSpec((1, d), lambda j: (0, j)),
        ],
        out_specs=pl.BlockSpec((b, d), lambda j: (0, j)),
        out_shape=jax.ShapeDtypeStruct((b, n), f32),
        compiler_params=pltpu.CompilerParams(
            dimension_semantics=("arbitrary",), vmem_limit_bytes=VMEM_LIMIT),
        name="adaln_mod",
    )(c, w_ada, b_ada.reshape(1, n))


def _inproj_kernel(x_ref, mod_ref, wf_ref, wg_ref, wr_ref, gq_ref, gk_ref, bf_ref, cos_ref, sin_ref,
                   fq_ref, fk_ref, fv_ref, fog_ref, lf_ref, rq_ref, rk_ref, rv_ref, rg_ref):
    x = x_ref[0]
    inv = lax.rsqrt(jnp.mean(x * x, axis=-1, keepdims=True) + EPS)
    h = (x * inv * (1.0 + mod_ref[0, 1:2, :]) + mod_ref[0, 0:1, :]).astype(bf16)
    lo = _half_lane_mask()
    w = FOX_WIDTH

    def qk_norm(p, gain_ref):
        outs = []
        for g in range(w // LANES):
            v = p[:, g * LANES:(g + 1) * LANES]
            ss = _pair_sumsq(v, lo)
            outs.append(v * lax.rsqrt(ss * (1.0 / FOX_HEAD_DIM) + EPS))
        return (jnp.concatenate(outs, axis=-1) * gain_ref[...]).astype(bf16)

    fq_ref[0] = qk_norm(_dot(h, wf_ref[:, 0:w]), gq_ref)
    fk_ref[0] = qk_norm(_dot(h, wf_ref[:, w:2 * w]), gk_ref)
    fv_ref[0] = _dot(h, wf_ref[:, 2 * w:3 * w]).astype(bf16)
    fog_ref[0] = jax.nn.sigmoid(_dot(h, wf_ref[:, 3 * w:4 * w])).astype(bf16)

    z = _dot(h, wg_ref[...]) + bf_ref[...]
    log_f = jnp.minimum(z, 0.0) - jnp.log(1.0 + jnp.exp(-jnp.abs(z)))
    lf_ref[0] = log_f.T[0:FOX_HEADS, :]

    cos = cos_ref[...]
    sin = sin_ref[...]
    rw = RET_WIDTH

    def rope(p, scale):
        outs = []
        for g in range(RET_HEADS):
            v = p[:, g * LANES:(g + 1) * LANES]
            outs.append(v * cos + pltpu.roll(v, RET_HEAD_DIM // 2, 1) * sin)
        r = jnp.concatenate(outs, axis=-1)
        return (r * scale if scale != 1.0 else r).astype(bf16)

    rq_ref[0] = rope(_dot(h, wr_ref[:, 0:rw]), 1.0)
    rk_ref[0] = rope(_dot(h, wr_ref[:, rw:2 * rw]), RET_HEAD_DIM ** -0.5)
    rv_ref[0] = _dot(h, wr_ref[:, 2 * rw:3 * rw]).astype(bf16)
    gate = _dot(h, wr_ref[:, 3 * rw:4 * rw])
    rg_ref[0] = (gate * jax.nn.sigmoid(gate)).astype(bf16)


def _input_projection(x, mod, w_in, b_forget, q_gain, k_gain):
    b, s, d = x.shape
    tm = min(TOKEN_TILE, s)
    o_ff = 4 * FOX_WIDTH
    o_r = o_ff + FOX_HEADS
    w_fox = w_in[:, :o_ff].astype(bf16)
    w_fg = jnp.pad(w_in[:, o_ff:o_r], ((0, 0), (0, LANES - FOX_HEADS))).astype(bf16)
    w_ret = w_in[:, o_r:].astype(bf16)
    bias_f = jnp.pad(b_forget, (0, LANES - FOX_HEADS)).reshape(1, LANES)
    gq = (jnp.tile(q_gain, FOX_HEADS) * (LOG2E * FOX_HEAD_DIM ** -0.5)).reshape(1, FOX_WIDTH)
    gk = jnp.tile(k_gain, FOX_HEADS).reshape(1, FOX_WIDTH)

    half = RET_HEAD_DIM // 2
    pos = jnp.arange(s, dtype=f32)
    inv_freq = ROPE_BASE ** (-jnp.arange(0, RET_HEAD_DIM, 2, dtype=f32) / RET_HEAD_DIM)
    ang = pos[:, None] * inv_freq[None, :]
    cos_t = jnp.concatenate([jnp.cos(ang), jnp.cos(ang)], axis=-1)
    sin_t = jnp.concatenate([-jnp.sin(ang), jnp.sin(ang)], axis=-1)
    assert cos_t.shape == (s, 2 * half)

    tok = lambda width: pl.BlockSpec((1, tm, width), lambda i, j: (i, j, 0))
    full = lambda a: pl.BlockSpec(a.shape, lambda i, j: (0,) * a.ndim)
    act = lambda width: jax.ShapeDtypeStruct((b, s, width), bf16)
    return pl.pallas_call(
        _inproj_kernel,
        grid=(b, s // tm),
        in_specs=[
            tok(d),
            pl.BlockSpec((1, N_MOD, d), lambda i, j: (i, 0, 0)),
            full(w_fox), full(w_fg), full(w_ret), full(gq), full(gk), full(bias_f),
            pl.BlockSpec((tm, LANES), lambda i, j: (j, 0)),
            pl.BlockSpec((tm, LANES), lambda i, j: (j, 0)),
        ],
        out_specs=[
            tok(FOX_WIDTH), tok(FOX_WIDTH), tok(FOX_WIDTH), tok(FOX_WIDTH),
            pl.BlockSpec((1, FOX_HEADS, tm), lambda i, j: (i, 0, j)),
            tok(RET_WIDTH), tok(RET_WIDTH), tok(RET_WIDTH), tok(RET_WIDTH),
        ],
        out_shape=[
            act(FOX_WIDTH), act(FOX_WIDTH), act(FOX_WIDTH), act(FOX_WIDTH),
            jax.ShapeDtypeStruct((b, FOX_HEADS, s), f32),
            act(RET_WIDTH), act(RET_WIDTH), act(RET_WIDTH), act(RET_WIDTH),
        ],
        compiler_params=pltpu.CompilerParams(
            dimension_semantics=("arbitrary", "arbitrary"), vmem_limit_bytes=VMEM_LIMIT),
        name="in_proj",
    )(x, mod, w_fox, w_fg, w_ret, gq, gk, bias_f, cos_t, sin_t)


def _lane_cumsum(x, out_ref):
    rows, s = x.shape
    r = lax.broadcasted_iota(jnp.int32, (LANES, LANES), 0)
    c = lax.broadcasted_iota(jnp.int32, (LANES, LANES), 1)
    tri = (r <= c).astype(bf16)
    hi = x.astype(bf16).astype(f32)
    rest = x - hi
    mid = rest.astype(bf16).astype(f32)
    low = (rest - mid).astype(bf16).astype(f32)
    n = s // LANES
    chunks = [slice(ch * LANES, (ch + 1) * LANES) for ch in range(n)]
    pieces = jnp.concatenate([g[:, sl] for g in (hi, mid, low) for sl in chunks], axis=0)
    rhs = jnp.concatenate([tri, jnp.ones((LANES, LANES), bf16)], axis=1)
    part = _dot(pieces.astype(bf16), rhs)
    offset = jnp.zeros((rows, LANES), f32)
    for ch, sl in enumerate(chunks):
        hi_c, mid_c, low_c = (part[(g * n + ch) * rows:(g * n + ch + 1) * rows] for g in range(3))
        both = hi_c + mid_c + low_c
        out_ref[:, sl] = both[:, 0:LANES] + offset
        offset = offset + both[:, LANES:]


def _fox_kernel(q_ref, k_ref, v_ref, lf_ref, og_ref, gain_ref, o_ref, cum_ref, s_ref, p_ref):
    pair = pl.program_id(1)
    s = q_ref.shape[1]
    t = min(ATTN_TILE, s)

    @pl.when(pair == 0)
    def _():
        _lane_cumsum(lf_ref[0] * LOG2E, cum_ref)

    lo = _half_lane_mask()
    row = lax.broadcasted_iota(jnp.int32, (t, t), 0)
    col = lax.broadcasted_iota(jnp.int32, (t, t), 1)
    causal = row >= col
    cum_a = cum_ref[pl.ds(2 * pair, 1), :]
    cum_b = cum_ref[pl.ds(2 * pair + 1, 1), :]
    nq = s // t
    row_max, row_sum = {}, {}

    def score_pass(qi):
        q = q_ref[0, qi * t:(qi + 1) * t, :]
        zero = jnp.zeros_like(q)
        q2 = jnp.concatenate([jnp.where(lo, q, zero), jnp.where(lo, zero, q)], axis=0)
        m_run = jnp.full((2 * t, LANES), MASK_VALUE, f32)
        for kb in range(qi + 1):
            ks = slice(kb * t, (kb + 1) * t)
            sc = _dot_nt(q2, k_ref[0, ks, :])
            sa = sc[0:t] - cum_a[:, ks]
            sb = sc[t:] - cum_b[:, ks]
            if kb == qi:
                sa = jnp.where(causal, sa, MASK_VALUE)
                sb = jnp.where(causal, sb, MASK_VALUE)
            sc = jnp.concatenate([sa, sb], axis=0)
            s_ref[qi % 2, :, ks] = sc
            for j in range(t // LANES):
                m_run = jnp.maximum(m_run, sc[:, j * LANES:(j + 1) * LANES])
            yield
        row_max[qi] = jnp.max(m_run, axis=-1, keepdims=True)

    def prob_pass(qi):
        m = row_max.pop(qi)
        for kb in range(qi + 1):
            ks = slice(kb * t, (kb + 1) * t)
            p_ref[qi % 2, :, ks] = jnp.exp2(s_ref[qi % 2, :, ks] - m).astype(bf16)
            yield

    def finish(qi):
        rows = slice(qi * t, (qi + 1) * t)
        kext = (qi + 1) * t
        v1 = jnp.concatenate([v_ref[0, 0:kext, :], jnp.ones((kext, LANES), bf16)], axis=1)
        pv = _dot(p_ref[qi % 2, :, 0:kext], v1)
        o2 = pv[:, 0:LANES] / pv[:, LANES:]
        out = jnp.where(lo, o2[0:t], o2[t:])
        ss = _pair_sumsq(out, lo)
        out = out * lax.rsqrt(ss * (1.0 / FOX_HEAD_DIM) + EPS) * gain_ref[0]
        o_ref[0, rows, :] = (out * og_ref[0, rows, :].astype(f32)).astype(bf16)

    def interleave(*passes):
        live = list(passes)
        while live:
            live = [g for g in live if next(g, StopIteration) is not StopIteration]

    interleave(score_pass(nq - 1))
    for qi in reversed(range(nq)):
        if qi + 1 < nq:
            finish(qi + 1)
        interleave(*([score_pass(qi - 1)] if qi >= 1 else []), prob_pass(qi))
    finish(0)


def _fox_attention(fq, fk, fv, fog, log_f, out_gain):
    b, s, _ = fq.shape
    t = min(ATTN_TILE, s)
    pairs = FOX_WIDTH // LANES
    gain = out_gain.reshape(pairs, 1, LANES)
    seq = pl.BlockSpec((1, s, LANES), lambda i, p: (i, 0, p))
    return pl.pallas_call(
        _fox_kernel,
        grid=(b, pairs),
        in_specs=[
            seq, seq, seq,
            pl.BlockSpec((1, FOX_HEADS, s), lambda i, p: (i, 0, 0)),
            seq,
            pl.BlockSpec((1, 1, LANES), lambda i, p: (p, 0, 0)),
        ],
        out_specs=seq,
        out_shape=jax.ShapeDtypeStruct((b, s, FOX_WIDTH), bf16),
        scratch_shapes=[
            pltpu.VMEM((FOX_HEADS, s), f32),
            pltpu.VMEM((2, 2 * t, s), f32),
            pltpu.VMEM((2, 2 * t, s), bf16),
        ],
        compiler_params=pltpu.CompilerParams(
            dimension_semantics=("arbitrary", "arbitrary"), vmem_limit_bytes=VMEM_LIMIT),
        name="fox_attention",
    )(fq, fk, fv, log_f, fog, gain)


def _ret_kernel(q_ref, k_ref, v_ref, g_ref, cst_ref, gain_ref, o_ref, inner_ref, kv_ref, st_ref):
    s = q_ref.shape[1]
    c = RET_CHUNK
    heads = [slice(hd * RET_HEAD_DIM, (hd + 1) * RET_HEAD_DIM) for hd in range(RET_HEADS)]
    chunks = [slice(ci * c, (ci + 1) * c) for ci in range(s // c)]

    for hd, cs in enumerate(heads):
        for ci, rs in enumerate(chunks):
            k = k_ref[0, rs, cs]
            inner_ref[rs, cs] = (_dot_nt(q_ref[0, rs, cs], k) * cst_ref[hd, 0]).astype(bf16)
            kz = (k.astype(f32) * cst_ref[hd, 2]).T.astype(bf16)
            kv_ref[ci, :, cs] = _dot(kz, v_ref[0, rs, cs])

    for hd, cs in enumerate(heads):
        state = jnp.zeros((RET_HEAD_DIM, RET_HEAD_DIM), f32)
        for ci in range(len(chunks)):
            st_ref[ci, :, cs] = state.astype(bf16)
            state = state * cst_ref[hd, 3] + kv_ref[ci, :, cs]

    for hd, cs in enumerate(heads):
        for ci, rs in enumerate(chunks):
            q_xi = (q_ref[0, rs, cs].astype(f32) * cst_ref[hd, 1]).astype(bf16)
            lhs = jnp.concatenate([inner_ref[rs, cs], q_xi], axis=1)
            rhs = jnp.concatenate([v_ref[0, rs, cs], st_ref[ci, :, cs]], axis=0)
            out = _dot(lhs, rhs)
            inv = lax.rsqrt(jnp.mean(out * out, axis=-1, keepdims=True) + EPS)
            gate = g_ref[0, rs, cs].astype(f32)
            o_ref[0, rs, cs] = (out * inv * gain_ref[:, cs] * gate).astype(bf16)


def _retention_constants():
    c = RET_CHUNK
    log_g = jnp.log(1.0 - 2.0 ** (-5.0 - jnp.arange(RET_HEADS, dtype=f32)))
    n = jnp.arange(c, dtype=f32)
    diff = n[:, None] - n[None, :]
    mask = jnp.where(diff[None] >= 0, jnp.exp(jnp.maximum(diff, 0.0)[None] * log_g[:, None, None]), 0.0)
    xi = jnp.exp((n[None, :] + 1.0) * log_g[:, None])
    zeta = jnp.exp((c - 1.0 - n[None, :]) * log_g[:, None])
    g_chunk = jnp.exp(c * log_g)
    bc = lambda rows: jnp.broadcast_to(rows[:, :, None], (RET_HEADS, c, c))
    return jnp.stack([mask, bc(xi), bc(zeta), jnp.broadcast_to(g_chunk[:, None, None], (RET_HEADS, c, c))], axis=1)


def _retention(rq, rk, rv, rg, out_gain):
    b, s, _ = rq.shape
    cst = _retention_constants()
    gain = out_gain.reshape(1, RET_WIDTH)
    seq = pl.BlockSpec((1, s, RET_WIDTH), lambda i: (i, 0, 0))
    return pl.pallas_call(
        _ret_kernel,
        grid=(b,),
        in_specs=[seq, seq, seq, seq,
                  pl.BlockSpec(cst.shape, lambda i: (0, 0, 0, 0)),
                  pl.BlockSpec(gain.shape, lambda i: (0, 0))],
        out_specs=seq,
        out_shape=jax.ShapeDtypeStruct((b, s, RET_WIDTH), bf16),
        scratch_shapes=[
            pltpu.VMEM((s, RET_WIDTH), bf16),
            pltpu.VMEM((s // RET_CHUNK, RET_HEAD_DIM, RET_WIDTH), f32),
            pltpu.VMEM((s // RET_CHUNK, RET_HEAD_DIM, RET_WIDTH), bf16),
        ],
        compiler_params=pltpu.CompilerParams(
            dimension_semantics=("arbitrary",), vmem_limit_bytes=VMEM_LIMIT),
        name="retention",
    )(rq, rk, rv, rg, cst, gain)


def _out_mlp_kernel(x_ref, mf_ref, mr_ref, mod_ref, wo_ref, w1_ref, w2_ref, o_ref):
    gate_m = mod_ref[0, 2:3, :]
    shift_f = mod_ref[0, 3:4, :]
    scale_f = mod_ref[0, 4:5, :]
    gate_f = mod_ref[0, 5:6, :]
    mixed = _dot(mf_ref[0], wo_ref[0:FOX_WIDTH, :]) + _dot(mr_ref[0], wo_ref[FOX_WIDTH:, :])
    x1 = x_ref[0] + gate_m * mixed
    inv = lax.rsqrt(jnp.mean(x1 * x1, axis=-1, keepdims=True) + EPS)
    h = (x1 * inv * (1.0 + scale_f) + shift_f).astype(bf16)
    d_ff = w1_ref.shape[1]
    fc = min(FF_CHUNK, d_ff)
    y = jnp.zeros(x1.shape, f32)
    for j in range(d_ff // fc):
        u = jnp.maximum(_dot(h, w1_ref[:, j * fc:(j + 1) * fc]), 0.0)
        y = y + _dot((u * u).astype(bf16), w2_ref[j * fc:(j + 1) * fc, :])
    o_ref[0] = x1 + gate_f * y


def _out_mlp(x, mixed_fox, mixed_ret, mod, w_out, w_mlp_in, w_mlp_out):
    b, s, d = x.shape
    tm = min(TOKEN_TILE, s)
    wo = w_out.astype(bf16)
    w1 = w_mlp_in.astype(bf16)
    w2 = w_mlp_out.astype(bf16)
    tok = lambda width: pl.BlockSpec((1, tm, width), lambda i, j: (i, j, 0))
    resident = lambda a: pl.BlockSpec(a.shape, lambda i, j: (0, 0), pipeline_mode=pl.Buffered(1))
    return pl.pallas_call(
        _out_mlp_kernel,
        grid=(b, s // tm),
        in_specs=[
            tok(d), tok(FOX_WIDTH), tok(RET_WIDTH),
            pl.BlockSpec((1, N_MOD, d), lambda i, j: (i, 0, 0)),
            resident(wo), resident(w1), resident(w2),
        ],
        out_specs=tok(d),
        out_shape=jax.ShapeDtypeStruct((b, s, d), f32),
        compiler_params=pltpu.CompilerParams(
            dimension_semantics=("arbitrary", "arbitrary"), vmem_limit_bytes=VMEM_LIMIT),
        name="out_mlp",
    )(x, mixed_fox, mixed_ret, mod, wo, w1, w2)


def kernel(x, c, w_ada, b_ada, w_in, b_forget, q_norm_gain, k_norm_gain, fox_out_gain, ret_out_gain,
           w_out, w_mlp_in, w_mlp_out):
    b, s, d = x.shape
    for l in range(w_ada.shape[0]):
        mod = _modulation(c, w_ada[l], b_ada[l]).reshape(b, N_MOD, d)
        fq, fk, fv, fog, log_f, rq, rk, rv, rg = _input_projection(
            x, mod, w_in[l], b_forget[l], q_norm_gain[l], k_norm_gain[l])
        mixed_fox = _fox_attention(fq, fk, fv, fog, log_f, fox_out_gain[l])
        mixed_ret = _retention(rq, rk, rv, rg, ret_out_gain[l])
        x = _out_mlp(x, mixed_fox, mixed_ret, mod, w_out[l], w_mlp_in[l], w_mlp_out[l])
    return x
```

```python
import jax
import jax.numpy as jnp
from jax import lax
from jax.experimental import pallas as pl
from jax.experimental.pallas import tpu as pltpu

FOX_HEADS = 8
FOX_HEAD_DIM = 64
FOX_WIDTH = FOX_HEADS * FOX_HEAD_DIM
RET_HEADS = 4
RET_HEAD_DIM = 128
RET_WIDTH = RET_HEADS * RET_HEAD_DIM
RET_CHUNK = 128
ROPE_BASE = 10000.0
EPS = 1e-6
N_MOD = 6

LANES = 128
VMEM_LIMIT = 56 * 1024 * 1024
MASK_VALUE = -1e30
LOG2E = 1.4426950408889634

TOKEN_TILE = 512
ATTN_TILE = 256
FF_CHUNK = 1024

bf16 = jnp.bfloat16
f32 = jnp.float32


def _dot(a, b):
    return jnp.dot(a, b, preferred_element_type=f32)


def _dot_nt(a, b):
    return lax.dot_general(a, b, (((1,), (1,)), ((), ())), preferred_element_type=f32)


def _half_lane_mask():
    return lax.broadcasted_iota(jnp.int32, (1, LANES), 1) < FOX_HEAD_DIM


def _pair_sumsq(v, lo):
    sq = v * v
    s_lo = jnp.sum(jnp.where(lo, sq, 0.0), axis=-1, keepdims=True)
    s_hi = jnp.sum(jnp.where(lo, 0.0, sq), axis=-1, keepdims=True)
    return jnp.where(lo, s_lo, s_hi)


def _mod_kernel(c_ref, w_ref, b_ref, o_ref):
    c = c_ref[...]
    c_act = (c * jax.nn.sigmoid(c)).astype(bf16)
    o_ref[...] = _dot(c_act, w_ref[...].astype(bf16)) + b_ref[...]


def _modulation(c, w_ada, b_ada):
    b, d = c.shape
    n = w_ada.shape[1]
    return pl.pallas_call(
        _mod_kernel,
        grid=(n // d,),
        in_specs=[
            pl.BlockSpec((b, d), lambda j: (0, 0)),
            pl.BlockSpec((d, d), lambda j: (0, j)),
            pl.BlockSpec((1, d), lambda j: (0, j)),
        ],
        out_specs=pl.BlockSpec((b, d), lambda j: (0, j)),
        out_shape=jax.ShapeDtypeStruct((b, n), f32),
        compiler_params=pltpu.CompilerParams(
            dimension_semantics=("arbitrary",), vmem_limit_bytes=VMEM_LIMIT),
        name="adaln_mod",
    )(c, w_ada, b_ada.reshape(1, n))


def _shift_proj_kernel(shift_ref, wf_ref, wg_ref, wr_ref, bf_ref, sf_ref, sg_ref, sr_ref):
    sh = shift_ref[...].astype(bf16)
    sf_ref[...] = _dot(sh, wf_ref[...])
    sg_ref[...] = _dot(sh, wg_ref[...]) + bf_ref[...]
    sr_ref[...] = _dot(sh, wr_ref[...])


def _inproj_kernel(x_ref, mod_ref, wf_ref, wg_ref, wr_ref, sf_ref, sg_ref, sr_ref, gq_ref, gk_ref, cos_ref, sin_ref,
                   fq_ref, fk_ref, fv_ref, fog_ref, lf_ref, rq_ref, rk_ref, rv_ref, rg_ref):
    x = x_ref[0]
    xs = (x * (1.0 + mod_ref[0, 1:2, :])).astype(bf16)
    inv = lax.rsqrt(jnp.mean(x * x, axis=-1, keepdims=True) + EPS)
    lo = _half_lane_mask()
    w = FOX_WIDTH
    rw = RET_WIDTH

    def proj(w_ref, s_ref, start, n):
        return _dot(xs, w_ref[:, start:start + n]) * inv + s_ref[0, :, start:start + n]

    def qk_norm(p, gain_ref):
        outs = []
        for g in range(w // LANES):
            v = p[:, g * LANES:(g + 1) * LANES]
            ss = _pair_sumsq(v, lo)
            outs.append(v * lax.rsqrt(ss * (1.0 / FOX_HEAD_DIM) + EPS))
        return (jnp.concatenate(outs, axis=-1) * gain_ref[...]).astype(bf16)

    cos = cos_ref[...]
    sin = sin_ref[...]

    def rope(p, scale):
        outs = []
        for g in range(RET_HEADS):
            v = p[:, g * LANES:(g + 1) * LANES]
            outs.append(v * cos + pltpu.roll(v, RET_HEAD_DIM // 2, 1) * sin)
        r = jnp.concatenate(outs, axis=-1)
        return (r * scale if scale != 1.0 else r).astype(bf16)

    fq_ref[0] = qk_norm(proj(wf_ref, sf_ref, 0, w), gq_ref)
    fk_ref[0] = qk_norm(proj(wf_ref, sf_ref, w, w), gk_ref)
    fog_ref[0] = jax.nn.sigmoid(proj(wf_ref, sf_ref, 3 * w, w)).astype(bf16)

    z = proj(wg_ref, sg_ref, 0, LANES)
    log_f = jnp.minimum(z, 0.0) - jnp.log(1.0 + jnp.exp(-jnp.abs(z)))
    lf_ref[0] = log_f.T[0:FOX_HEADS, :]

    rq_ref[0] = rope(proj(wr_ref, sr_ref, 0, rw), 1.0)
    rk_ref[0] = rope(proj(wr_ref, sr_ref, rw, rw), RET_HEAD_DIM ** -0.5)
    gate = proj(wr_ref, sr_ref, 3 * rw, rw)
    rg_ref[0] = (gate * jax.nn.sigmoid(gate)).astype(bf16)
    rv_ref[0] = proj(wr_ref, sr_ref, 2 * rw, rw).astype(bf16)
    fv_ref[0] = proj(wf_ref, sf_ref, 2 * w, w).astype(bf16)


def _input_projection(x, mod2d, w_in, b_forget, q_gain, k_gain):
    b, s, d = x.shape
    mod = mod2d.reshape(b, N_MOD, d)
    tm = min(2 * TOKEN_TILE, s)
    o_ff = 4 * FOX_WIDTH
    o_r = o_ff + FOX_HEADS
    w_fox = w_in[:, :o_ff].astype(bf16)
    w_fg = jnp.pad(w_in[:, o_ff:o_r], ((0, 0), (0, LANES - FOX_HEADS))).astype(bf16)
    w_ret = w_in[:, o_r:].astype(bf16)
    bias_f = jnp.pad(b_forget, (0, LANES - FOX_HEADS)).reshape(1, LANES)
    gq = (jnp.tile(q_gain, FOX_HEADS) * (LOG2E * FOX_HEAD_DIM ** -0.5)).reshape(1, FOX_WIDTH)
    gk = jnp.tile(k_gain, FOX_HEADS).reshape(1, FOX_WIDTH)

    half = RET_HEAD_DIM // 2
    pos = jnp.arange(s, dtype=f32)
    inv_freq = ROPE_BASE ** (-jnp.arange(0, RET_HEAD_DIM, 2, dtype=f32) / RET_HEAD_DIM)
    ang = pos[:, None] * inv_freq[None, :]
    cos_t = jnp.concatenate([jnp.cos(ang), jnp.cos(ang)], axis=-1)
    sin_t = jnp.concatenate([-jnp.sin(ang), jnp.sin(ang)], axis=-1)
    assert cos_t.shape == (s, 2 * half)

    whole = lambda a: pl.BlockSpec(a.shape, lambda i: (0,) * a.ndim)
    shift_w = pl.pallas_call(
        _shift_proj_kernel,
        grid=(1,),
        in_specs=[pl.BlockSpec((b, d), lambda i: (0, 0)), whole(w_fox), whole(w_fg), whole(w_ret), whole(bias_f)],
        out_specs=[pl.BlockSpec((b, n), lambda i: (0, 0)) for n in (o_ff, LANES, 4 * RET_WIDTH)],
        out_shape=[jax.ShapeDtypeStruct((b, n), f32) for n in (o_ff, LANES, 4 * RET_WIDTH)],
        compiler_params=pltpu.CompilerParams(
            dimension_semantics=("arbitrary",), vmem_limit_bytes=VMEM_LIMIT),
        name="shift_proj",
    )(mod2d, w_fox, w_fg, w_ret, bias_f)
    s_fox, s_fg, s_ret = (a.reshape(b, 1, a.shape[1]) for a in shift_w)

    tok = lambda width: pl.BlockSpec((1, tm, width), lambda i, j: (i, j, 0))
    full = lambda a: pl.BlockSpec(a.shape, lambda i, j: (0,) * a.ndim, pipeline_mode=pl.Buffered(1))
    per_seq = lambda a: pl.BlockSpec((1,) + a.shape[1:], lambda i, j: (i, 0, 0))
    act = lambda width: jax.ShapeDtypeStruct((b, s, width), bf16)
    return pl.pallas_call(
        _inproj_kernel,
        grid=(b, s // tm),
        in_specs=[
            tok(d),
            pl.BlockSpec((1, N_MOD, d), lambda i, j: (i, 0, 0)),
            full(w_fox), full(w_fg), full(w_ret), per_seq(s_fox), per_seq(s_fg), per_seq(s_ret), full(gq), full(gk),
            pl.BlockSpec((tm, LANES), lambda i, j: (j, 0)),
            pl.BlockSpec((tm, LANES), lambda i, j: (j, 0)),
        ],
        out_specs=[
            tok(FOX_WIDTH), tok(FOX_WIDTH), tok(FOX_WIDTH), tok(FOX_WIDTH),
            pl.BlockSpec((1, FOX_HEADS, tm), lambda i, j: (i, 0, j)),
            tok(RET_WIDTH), tok(RET_WIDTH), tok(RET_WIDTH), tok(RET_WIDTH),
        ],
        out_shape=[
            act(FOX_WIDTH), act(FOX_WIDTH), act(FOX_WIDTH), act(FOX_WIDTH),
            jax.ShapeDtypeStruct((b, FOX_HEADS, s), f32),
            act(RET_WIDTH), act(RET_WIDTH), act(RET_WIDTH), act(RET_WIDTH),
        ],
        compiler_params=pltpu.CompilerParams(
            dimension_semantics=("arbitrary", "arbitrary"), vmem_limit_bytes=VMEM_LIMIT),
        name="in_proj",
    )(x, mod, w_fox, w_fg, w_ret, s_fox, s_fg, s_ret, gq, gk, cos_t, sin_t)


def _lane_cumsum(x, out_ref):
    rows, s = x.shape
    r = lax.broadcasted_iota(jnp.int32, (LANES, LANES), 0)
    c = lax.broadcasted_iota(jnp.int32, (LANES, LANES), 1)
    tri = (r <= c).astype(bf16)
    hi = x.astype(bf16).astype(f32)
    rest = x - hi
    mid = rest.astype(bf16).astype(f32)
    low = (rest - mid).astype(bf16).astype(f32)
    n = s // LANES
    chunks = [slice(ch * LANES, (ch + 1) * LANES) for ch in range(n)]
    pieces = jnp.concatenate([g[:, sl] for g in (hi, mid, low) for sl in chunks], axis=0)
    rhs = jnp.concatenate([tri, jnp.ones((LANES, LANES), bf16)], axis=1)
    part = _dot(pieces.astype(bf16), rhs)
    offset = jnp.zeros((rows, LANES), f32)
    for ch, sl in enumerate(chunks):
        hi_c, mid_c, low_c = (part[(g * n + ch) * rows:(g * n + ch + 1) * rows] for g in range(3))
        both = hi_c + mid_c + low_c
        out_ref[:, sl] = both[:, 0:LANES] + offset
        offset = offset + both[:, LANES:]


def _fox_kernel(q_ref, k_ref, v_ref, lf_ref, og_ref, gain_ref, o_ref, cum_ref, s_ref, p_ref):
    pair = pl.program_id(1)
    s = q_ref.shape[1]
    t = min(ATTN_TILE, s)

    @pl.when(pair == 0)
    def _():
        _lane_cumsum(lf_ref[0] * LOG2E, cum_ref)

    lo = _half_lane_mask()
    row = lax.broadcasted_iota(jnp.int32, (t, t), 0)
    col = lax.broadcasted_iota(jnp.int32, (t, t), 1)
    causal = row >= col
    cum_a = cum_ref[pl.ds(2 * pair, 1), :]
    cum_b = cum_ref[pl.ds(2 * pair + 1, 1), :]
    nq = s // t
    row_max = {}

    def score_pass(qi):
        q = q_ref[0, qi * t:(qi + 1) * t, :]
        zero = jnp.zeros_like(q)
        q2 = jnp.concatenate([jnp.where(lo, q, zero), jnp.where(lo, zero, q)], axis=0)
        m_run = jnp.full((2 * t, LANES), MASK_VALUE, f32)
        for kb in range(qi + 1):
            ks = slice(kb * t, (kb + 1) * t)
            sc = _dot_nt(q2, k_ref[0, ks, :])
            sa = sc[0:t] - cum_a[:, ks]
            sb = sc[t:] - cum_b[:, ks]
            if kb == qi:
                sa = jnp.where(causal, sa, MASK_VALUE)
                sb = jnp.where(causal, sb, MASK_VALUE)
            sc = jnp.concatenate([sa, sb], axis=0)
            s_ref[qi % 2, :, ks] = sc
            for j in range(t // LANES):
                m_run = jnp.maximum(m_run, sc[:, j * LANES:(j + 1) * LANES])
            yield
        row_max[qi] = jnp.max(m_run, axis=-1, keepdims=True)

    def prob_pass(qi):
        m = row_max.pop(qi)
        for kb in range(qi + 1):
            ks = slice(kb * t, (kb + 1) * t)
            p_ref[qi % 2, :, ks] = jnp.exp2(s_ref[qi % 2, :, ks] - m).astype(bf16)
            yield

    def finish(qi):
        rows = slice(qi * t, (qi + 1) * t)
        kext = (qi + 1) * t
        v1 = jnp.concatenate([v_ref[0, 0:kext, :], jnp.ones((kext, LANES), bf16)], axis=1)
        pv = _dot(p_ref[qi % 2, :, 0:kext], v1)
        o2 = pv[:, 0:LANES] / pv[:, LANES:]
        out = jnp.where(lo, o2[0:t], o2[t:])
        ss = _pair_sumsq(out, lo)
        out = out * lax.rsqrt(ss * (1.0 / FOX_HEAD_DIM) + EPS) * gain_ref[0]
        o_ref[0, rows, :] = (out * og_ref[0, rows, :].astype(f32)).astype(bf16)

    def interleave(*passes):
        live = list(passes)
        while live:
            live = [g for g in live if next(g, StopIteration) is not StopIteration]

    interleave(score_pass(nq - 1))
    for qi in reversed(range(nq)):
        if qi + 1 < nq:
            finish(qi + 1)
        interleave(*([score_pass(qi - 1)] if qi >= 1 else []), prob_pass(qi))
    finish(0)


def _fox_attention(fq, fk, fv, fog, log_f, out_gain):
    b, s, _ = fq.shape
    t = min(ATTN_TILE, s)
    pairs = FOX_WIDTH // LANES
    gain = out_gain.reshape(pairs, 1, LANES)
    seq = pl.BlockSpec((1, s, LANES), lambda i, p: (i, 0, p))
    return pl.pallas_call(
        _fox_kernel,
        grid=(b, pairs),
        in_specs=[
            seq, seq, seq,
            pl.BlockSpec((1, FOX_HEADS, s), lambda i, p: (i, 0, 0)),
            seq,
            pl.BlockSpec((1, 1, LANES), lambda i, p: (p, 0, 0)),
        ],
        out_specs=seq,
        out_shape=jax.ShapeDtypeStruct((b, s, FOX_WIDTH), bf16),
        scratch_shapes=[
            pltpu.VMEM((FOX_HEADS, s), f32),
            pltpu.VMEM((2, 2 * t, s), f32),
            pltpu.VMEM((2, 2 * t, s), bf16),
        ],
        compiler_params=pltpu.CompilerParams(
            dimension_semantics=("arbitrary", "arbitrary"), vmem_limit_bytes=VMEM_LIMIT),
        name="fox_attention",
    )(fq, fk, fv, log_f, fog, gain)


def _ret_kernel(q_ref, k_ref, v_ref, g_ref, cst_ref, gain_ref, o_ref, inner_ref, kv_ref, st_ref):
    s = q_ref.shape[1]
    c = RET_CHUNK
    heads = [slice(hd * RET_HEAD_DIM, (hd + 1) * RET_HEAD_DIM) for hd in range(RET_HEADS)]
    chunks = [slice(ci * c, (ci + 1) * c) for ci in range(s // c)]

    for hd, cs in enumerate(heads):
        for ci, rs in enumerate(chunks):
            k = k_ref[0, rs, cs]
            inner_ref[rs, cs] = (_dot_nt(q_ref[0, rs, cs], k) * cst_ref[hd, 0]).astype(bf16)
            kz = (k.astype(f32) * cst_ref[hd, 2]).T.astype(bf16)
            kv_ref[ci, :, cs] = _dot(kz, v_ref[0, rs, cs])

    for hd, cs in enumerate(heads):
        state = jnp.zeros((RET_HEAD_DIM, RET_HEAD_DIM), f32)
        for ci in range(len(chunks)):
            st_ref[ci, :, cs] = state.astype(bf16)
            state = state * cst_ref[hd, 3] + kv_ref[ci, :, cs]

    for hd, cs in enumerate(heads):
        for ci, rs in enumerate(chunks):
            q_xi = (q_ref[0, rs, cs].astype(f32) * cst_ref[hd, 1]).astype(bf16)
            lhs = jnp.concatenate([inner_ref[rs, cs], q_xi], axis=1)
            rhs = jnp.concatenate([v_ref[0, rs, cs], st_ref[ci, :, cs]], axis=0)
            out = _dot(lhs, rhs)
            inv = lax.rsqrt(jnp.mean(out * out, axis=-1, keepdims=True) + EPS)
            gate = g_ref[0, rs, cs].astype(f32)
            o_ref[0, rs, cs] = (out * inv * gain_ref[:, cs] * gate).astype(bf16)


def _retention_constants():
    c = RET_CHUNK
    log_g = jnp.log(1.0 - 2.0 ** (-5.0 - jnp.arange(RET_HEADS, dtype=f32)))
    n = jnp.arange(c, dtype=f32)
    diff = n[:, None] - n[None, :]
    mask = jnp.where(diff[None] >= 0, jnp.exp(jnp.maximum(diff, 0.0)[None] * log_g[:, None, None]), 0.0)
    xi = jnp.exp((n[None, :] + 1.0) * log_g[:, None])
    zeta = jnp.exp((c - 1.0 - n[None, :]) * log_g[:, None])
    g_chunk = jnp.exp(c * log_g)
    bc = lambda rows: jnp.broadcast_to(rows[:, :, None], (RET_HEADS, c, c))
    return jnp.stack([mask, bc(xi), bc(zeta), jnp.broadcast_to(g_chunk[:, None, None], (RET_HEADS, c, c))], axis=1)


def _retention(rq, rk, rv, rg, out_gain):
    b, s, _ = rq.shape
    cst = _retention_constants()
    gain = out_gain.reshape(1, RET_WIDTH)
    seq = pl.BlockSpec((1, s, RET_WIDTH), lambda i: (i, 0, 0))
    return pl.pallas_call(
        _ret_kernel,
        grid=(b,),
        in_specs=[seq, seq, seq, seq,
                  pl.BlockSpec(cst.shape, lambda i: (0, 0, 0, 0)),
                  pl.BlockSpec(gain.shape, lambda i: (0, 0))],
        out_specs=seq,
        out_shape=jax.ShapeDtypeStruct((b, s, RET_WIDTH), bf16),
        scratch_shapes=[
            pltpu.VMEM((s, RET_WIDTH), bf16),
            pltpu.VMEM((s // RET_CHUNK, RET_HEAD_DIM, RET_WIDTH), f32),
            pltpu.VMEM((s // RET_CHUNK, RET_HEAD_DIM, RET_WIDTH), bf16),
        ],
        compiler_params=pltpu.CompilerParams(
            dimension_semantics=("arbitrary",), vmem_limit_bytes=VMEM_LIMIT),
        name="retention",
    )(rq, rk, rv, rg, cst, gain)


def _out_mlp_kernel(x_ref, mf_ref, mr_ref, mod_ref, wo_ref, w1_ref, w2_ref, o_ref):
    gate_m = mod_ref[0, 2:3, :]
    shift_f = mod_ref[0, 3:4, :]
    scale_f = mod_ref[0, 4:5, :]
    gate_f = mod_ref[0, 5:6, :]
    mixed = _dot(mf_ref[0], wo_ref[0:FOX_WIDTH, :]) + _dot(mr_ref[0], wo_ref[FOX_WIDTH:, :])
    x1 = x_ref[0] + gate_m * mixed
    inv = lax.rsqrt(jnp.mean(x1 * x1, axis=-1, keepdims=True) + EPS)
    h = (x1 * inv * (1.0 + scale_f) + shift_f).astype(bf16)
    d_ff = w1_ref.shape[1]
    fc = min(FF_CHUNK, d_ff)
    y = jnp.zeros(x1.shape, f32)
    for j in range(d_ff // fc):
        u = jnp.maximum(_dot(h, w1_ref[:, j * fc:(j + 1) * fc]), 0.0)
        y = y + _dot((u * u).astype(bf16), w2_ref[j * fc:(j + 1) * fc, :])
    o_ref[0] = x1 + gate_f * y


def _out_mlp(x, mixed_fox, mixed_ret, mod, w_out, w_mlp_in, w_mlp_out):
    b, s, d = x.shape
    tm = min(TOKEN_TILE, s)
    wo = w_out.astype(bf16)
    w1 = w_mlp_in.astype(bf16)
    w2 = w_mlp_out.astype(bf16)
    tok = lambda width: pl.BlockSpec((1, tm, width), lambda i, j: (i, j, 0))
    resident = lambda a: pl.BlockSpec(a.shape, lambda i, j: (0, 0), pipeline_mode=pl.Buffered(1))
    return pl.pallas_call(
        _out_mlp_kernel,
        grid=(b, s // tm),
        in_specs=[
            tok(d), tok(FOX_WIDTH), tok(RET_WIDTH),
            pl.BlockSpec((1, N_MOD, d), lambda i, j: (i, 0, 0)),
            resident(wo), resident(w1), resident(w2),
        ],
        out_specs=tok(d),
        out_shape=jax.ShapeDtypeStruct((b, s, d), f32),
        compiler_params=pltpu.CompilerParams(
            dimension_semantics=("arbitrary", "arbitrary"), vmem_limit_bytes=VMEM_LIMIT),
        name="out_mlp",
    )(x, mixed_fox, mixed_ret, mod, wo, w1, w2)


def kernel(x, c, w_ada, b_ada, w_in, b_forget, q_norm_gain, k_norm_gain, fox_out_gain, ret_out_gain,
           w_out, w_mlp_in, w_mlp_out):
    b, s, d = x.shape
    for l in range(w_ada.shape[0]):
        mod2d = _modulation(c, w_ada[l], b_ada[l])
        mod = mod2d.reshape(b, N_MOD, d)
        fq, fk, fv, fog, log_f, rq, rk, rv, rg = _input_projection(
            x, mod2d, w_in[l], b_forget[l], q_norm_gain[l], k_norm_gain[l])
        mixed_fox = _fox_attention(fq, fk, fv, fog, log_f, fox_out_gain[l])
        mixed_ret = _retention(rq, rk, rv, rg, ret_out_gain[l])
        x = _out_mlp(x, mixed_fox, mixed_ret, mod, w_out[l], w_mlp_in[l], w_mlp_out[l])
    return x
```

```python
import jax
import jax.numpy as jnp
from jax import lax
from jax.experimental import pallas as pl
from jax.experimental.pallas import tpu as pltpu

FOX_HEADS = 8
FOX_HEAD_DIM = 64
FOX_WIDTH = FOX_HEADS * FOX_HEAD_DIM
RET_HEADS = 4
RET_HEAD_DIM = 128
RET_WIDTH = RET_HEADS * RET_HEAD_DIM
RET_CHUNK = 128
ROPE_BASE = 10000.0
EPS = 1e-6
N_MOD = 6

LANES = 128
VMEM_LIMIT = 56 * 1024 * 1024
MASK_VALUE = -1e30
LOG2E = 1.4426950408889634

TOKEN_TILE = 512
ATTN_TILE = 256
FF_CHUNK = 1024

bf16 = jnp.bfloat16
f32 = jnp.float32


def _dot(a, b):
    return jnp.dot(a, b, preferred_element_type=f32)


def _dot_nt(a, b):
    return lax.dot_general(a, b, (((1,), (1,)), ((), ())), preferred_element_type=f32)


def _half_lane_mask():
    return lax.broadcasted_iota(jnp.int32, (1, LANES), 1) < FOX_HEAD_DIM


def _pair_sumsq(v, lo):
    sq = v * v
    s_lo = jnp.sum(jnp.where(lo, sq, 0.0), axis=-1, keepdims=True)
    s_hi = jnp.sum(jnp.where(lo, 0.0, sq), axis=-1, keepdims=True)
    return jnp.where(lo, s_lo, s_hi)


def _mod_kernel(c_ref, w_ref, b_ref, o_ref):
    c = c_ref[...]
    c_act = (c * jax.nn.sigmoid(c)).astype(bf16)
    o_ref[...] = _dot(c_act, w_ref[...].astype(bf16)) + b_ref[...]


def _modulation(c, w_ada, b_ada):
    b, d = c.shape
    n = w_ada.shape[1]
    return pl.pallas_call(
        _mod_kernel,
        grid=(n // d,),
        in_specs=[
            pl.BlockSpec((b, d), lambda j: (0, 0)),
            pl.BlockSpec((d, d), lambda j: (0, j)),
            pl.BlockSpec((1, d), lambda j: (0, j)),
        ],
        out_specs=pl.BlockSpec((b, d), lambda j: (0, j)),
        out_shape=jax.ShapeDtypeStruct((b, n), f32),
        compiler_params=pltpu.CompilerParams(
            dimension_semantics=("arbitrary",), vmem_limit_bytes=VMEM_LIMIT),
        name="adaln_mod",
    )(c, w_ada, b_ada.reshape(1, n))


def _inproj_kernel(x_ref, mod_ref, wf_ref, wg_ref, wr_ref, gq_ref, gk_ref, bf_ref, cos_ref, sin_ref,
                   fq_ref, fk_ref, fv_ref, fog_ref, lf_ref, rq_ref, rk_ref, rv_ref, rg_ref):
    x = x_ref[0]
    inv = lax.rsqrt(jnp.mean(x * x, axis=-1, keepdims=True) + EPS)
    h = (x * inv * (1.0 + mod_ref[0, 1:2, :]) + mod_ref[0, 0:1, :]).astype(bf16)
    lo = _half_lane_mask()
    w = FOX_WIDTH
    rw = RET_WIDTH

    def proj(w_ref, start, n):
        return _dot(h, w_ref[:, start:start + n])

    def qk_norm(p, gain_ref):
        outs = []
        for g in range(w // LANES):
            v = p[:, g * LANES:(g + 1) * LANES]
            ss = _pair_sumsq(v, lo)
            outs.append(v * lax.rsqrt(ss * (1.0 / FOX_HEAD_DIM) + EPS))
        return (jnp.concatenate(outs, axis=-1) * gain_ref[...]).astype(bf16)

    cos = cos_ref[...]
    sin = sin_ref[...]

    def rope(p, scale):
        outs = []
        for g in range(RET_HEADS):
            v = p[:, g * LANES:(g + 1) * LANES]
            outs.append(v * cos + pltpu.roll(v, RET_HEAD_DIM // 2, 1) * sin)
        r = jnp.concatenate(outs, axis=-1)
        return (r * scale if scale != 1.0 else r).astype(bf16)

    fq_ref[0] = qk_norm(proj(wf_ref, 0, w), gq_ref)
    fk_ref[0] = qk_norm(proj(wf_ref, w, w), gk_ref)
    fog_ref[0] = jax.nn.sigmoid(proj(wf_ref, 3 * w, w)).astype(bf16)

    z = proj(wg_ref, 0, LANES) + bf_ref[...]
    log_f = jnp.minimum(z, 0.0) - jnp.log(1.0 + jnp.exp(-jnp.abs(z)))
    lf_ref[0] = log_f.T[0:FOX_HEADS, :]

    rq_ref[0] = rope(proj(wr_ref, 0, rw), 1.0)
    rk_ref[0] = rope(proj(wr_ref, rw, rw), RET_HEAD_DIM ** -0.5)
    gate = proj(wr_ref, 3 * rw, rw)
    rg_ref[0] = (gate * jax.nn.sigmoid(gate)).astype(bf16)
    rv_ref[0] = proj(wr_ref, 2 * rw, rw).astype(bf16)
    fv_ref[0] = proj(wf_ref, 2 * w, w).astype(bf16)


def _input_projection(x, mod, w_in, b_forget, q_gain, k_gain):
    b, s, d = x.shape
    tm = min(TOKEN_TILE, s)
    o_ff = 4 * FOX_WIDTH
    o_r = o_ff + FOX_HEADS
    w_fox = w_in[:, :o_ff].astype(bf16)
    w_fg = jnp.pad(w_in[:, o_ff:o_r], ((0, 0), (0, LANES - FOX_HEADS))).astype(bf16)
    w_ret = w_in[:, o_r:].astype(bf16)
    bias_f = jnp.pad(b_forget, (0, LANES - FOX_HEADS)).reshape(1, LANES)
    gq = (jnp.tile(q_gain, FOX_HEADS) * (LOG2E * FOX_HEAD_DIM ** -0.5)).reshape(1, FOX_WIDTH)
    gk = jnp.tile(k_gain, FOX_HEADS).reshape(1, FOX_WIDTH)

    half = RET_HEAD_DIM // 2
    pos = jnp.arange(s, dtype=f32)
    inv_freq = ROPE_BASE ** (-jnp.arange(0, RET_HEAD_DIM, 2, dtype=f32) / RET_HEAD_DIM)
    ang = pos[:, None] * inv_freq[None, :]
    cos_t = jnp.concatenate([jnp.cos(ang), jnp.cos(ang)], axis=-1)
    sin_t = jnp.concatenate([-jnp.sin(ang), jnp.sin(ang)], axis=-1)
    assert cos_t.shape == (s, 2 * half)

    tok = lambda width: pl.BlockSpec((1, tm, width), lambda i, j: (i, j, 0))
    full = lambda a: pl.BlockSpec(a.shape, lambda i, j: (0,) * a.ndim)
    act = lambda width: jax.ShapeDtypeStruct((b, s, width), bf16)
    return pl.pallas_call(
        _inproj_kernel,
        grid=(b, s // tm),
        in_specs=[
            tok(d),
            pl.BlockSpec((1, N_MOD, d), lambda i, j: (i, 0, 0)),
            full(w_fox), full(w_fg), full(w_ret), full(gq), full(gk), full(bias_f),
            pl.BlockSpec((tm, LANES), lambda i, j: (j, 0)),
            pl.BlockSpec((tm, LANES), lambda i, j: (j, 0)),
        ],
        out_specs=[
            tok(FOX_WIDTH), tok(FOX_WIDTH), tok(FOX_WIDTH), tok(FOX_WIDTH),
            pl.BlockSpec((1, FOX_HEADS, tm), lambda i, j: (i, 0, j)),
            tok(RET_WIDTH), tok(RET_WIDTH), tok(RET_WIDTH), tok(RET_WIDTH),
        ],
        out_shape=[
            act(FOX_WIDTH), act(FOX_WIDTH), act(FOX_WIDTH), act(FOX_WIDTH),
            jax.ShapeDtypeStruct((b, FOX_HEADS, s), f32),
            act(RET_WIDTH), act(RET_WIDTH), act(RET_WIDTH), act(RET_WIDTH),
        ],
        compiler_params=pltpu.CompilerParams(
            dimension_semantics=("arbitrary", "arbitrary"), vmem_limit_bytes=VMEM_LIMIT),
        name="in_proj",
    )(x, mod, w_fox, w_fg, w_ret, gq, gk, bias_f, cos_t, sin_t)


def _lane_cumsum(x, out_ref):
    rows, s = x.shape
    r = lax.broadcasted_iota(jnp.int32, (LANES, LANES), 0)
    c = lax.broadcasted_iota(jnp.int32, (LANES, LANES), 1)
    tri = (r <= c).astype(bf16)
    hi = x.astype(bf16).astype(f32)
    rest = x - hi
    mid = rest.astype(bf16).astype(f32)
    low = (rest - mid).astype(bf16).astype(f32)
    n = s // LANES
    chunks = [slice(ch * LANES, (ch + 1) * LANES) for ch in range(n)]
    pieces = jnp.concatenate([g[:, sl] for g in (hi, mid, low) for sl in chunks], axis=0)
    rhs = jnp.concatenate([tri, jnp.ones((LANES, LANES), bf16)], axis=1)
    part = _dot(pieces.astype(bf16), rhs)
    offset = jnp.zeros((rows, LANES), f32)
    for ch, sl in enumerate(chunks):
        hi_c, mid_c, low_c = (part[(g * n + ch) * rows:(g * n + ch + 1) * rows] for g in range(3))
        both = hi_c + mid_c + low_c
        out_ref[:, sl] = both[:, 0:LANES] + offset
        offset = offset + both[:, LANES:]


def _fox_kernel(q_ref, k_ref, v_ref, lf_ref, og_ref, gain_ref, o_ref, cum_ref, s_ref, p_ref):
    pair = pl.program_id(1)
    s = q_ref.shape[1]
    t = min(ATTN_TILE, s)

    @pl.when(pair == 0)
    def _():
        _lane_cumsum(lf_ref[0] * LOG2E, cum_ref)

    lo = _half_lane_mask()
    row = lax.broadcasted_iota(jnp.int32, (t, t), 0)
    col = lax.broadcasted_iota(jnp.int32, (t, t), 1)
    causal = row >= col
    cum_a = cum_ref[pl.ds(2 * pair, 1), :]
    cum_b = cum_ref[pl.ds(2 * pair + 1, 1), :]
    nq = s // t
    row_max = {}

    def score_pass(qi):
        q = q_ref[0, qi * t:(qi + 1) * t, :]
        zero = jnp.zeros_like(q)
        q2 = jnp.concatenate([jnp.where(lo, q, zero), jnp.where(lo, zero, q)], axis=0)
        m_run = jnp.full((2 * t, LANES), MASK_VALUE, f32)
        for kb in range(qi + 1):
            ks = slice(kb * t, (kb + 1) * t)
            sc = _dot_nt(q2, k_ref[0, ks, :])
            sa = sc[0:t] - cum_a[:, ks]
            sb = sc[t:] - cum_b[:, ks]
            if kb == qi:
                sa = jnp.where(causal, sa, MASK_VALUE)
                sb = jnp.where(causal, sb, MASK_VALUE)
            sc = jnp.concatenate([sa, sb], axis=0)
            s_ref[qi % 2, :, ks] = sc
            for j in range(t // LANES):
                m_run = jnp.maximum(m_run, sc[:, j * LANES:(j + 1) * LANES])
            yield
        row_max[qi] = jnp.max(m_run, axis=-1, keepdims=True)

    def prob_pass(qi):
        m = row_max.pop(qi)
        for kb in range(qi + 1):
            ks = slice(kb * t, (kb + 1) * t)
            p_ref[qi % 2, :, ks] = jnp.exp2(s_ref[qi % 2, :, ks] - m).astype(bf16)
            yield

    def finish(qi):
        rows = slice(qi * t, (qi + 1) * t)
        kext = (qi + 1) * t
        v1 = jnp.concatenate([v_ref[0, 0:kext, :], jnp.ones((kext, LANES), bf16)], axis=1)
        pv = _dot(p_ref[qi % 2, :, 0:kext], v1)
        o2 = pv[:, 0:LANES] / pv[:, LANES:]
        out = jnp.where(lo, o2[0:t], o2[t:])
        ss = _pair_sumsq(out, lo)
        out = out * lax.rsqrt(ss * (1.0 / FOX_HEAD_DIM) + EPS) * gain_ref[0]
        o_ref[0, rows, :] = (out * og_ref[0, rows, :].astype(f32)).astype(bf16)

    def interleave(*passes):
        live = list(passes)
        while live:
            live = [g for g in live if next(g, StopIteration) is not StopIteration]

    interleave(score_pass(nq - 1))
    for qi in reversed(range(nq)):
        if qi + 1 < nq:
            finish(qi + 1)
        interleave(*([score_pass(qi - 1)] if qi >= 1 else []), prob_pass(qi))
    finish(0)


def _fox_attention(fq, fk, fv, fog, log_f, out_gain):
    b, s, _ = fq.shape
    t = min(ATTN_TILE, s)
    pairs = FOX_WIDTH // LANES
    gain = out_gain.reshape(pairs, 1, LANES)
    seq = pl.BlockSpec((1, s, LANES), lambda i, p: (i, 0, p))
    return pl.pallas_call(
        _fox_kernel,
        grid=(b, pairs),
        in_specs=[
            seq, seq, seq,
            pl.BlockSpec((1, FOX_HEADS, s), lambda i, p: (i, 0, 0)),
            seq,
            pl.BlockSpec((1, 1, LANES), lambda i, p: (p, 0, 0)),
        ],
        out_specs=seq,
        out_shape=jax.ShapeDtypeStruct((b, s, FOX_WIDTH), bf16),
        scratch_shapes=[
            pltpu.VMEM((FOX_HEADS, s), f32),
            pltpu.VMEM((2, 2 * t, s), f32),
            pltpu.VMEM((2, 2 * t, s), bf16),
        ],
        compiler_params=pltpu.CompilerParams(
            dimension_semantics=("arbitrary", "arbitrary"), vmem_limit_bytes=VMEM_LIMIT),
        name="fox_attention",
    )(fq, fk, fv, log_f, fog, gain)


def _ret_kernel(q_ref, k_ref, v_ref, g_ref, cst_ref, gain_ref, o_ref, inner_ref, kv_ref, st_ref):
    s = q_ref.shape[1]
    c = RET_CHUNK
    heads = [slice(hd * RET_HEAD_DIM, (hd + 1) * RET_HEAD_DIM) for hd in range(RET_HEADS)]
    chunks = [slice(ci * c, (ci + 1) * c) for ci in range(s // c)]

    for hd, cs in enumerate(heads):
        for ci, rs in enumerate(chunks):
            k = k_ref[0, rs, cs]
            inner_ref[rs, cs] = (_dot_nt(q_ref[0, rs, cs], k) * cst_ref[hd, 0]).astype(bf16)
            kz = (k.astype(f32) * cst_ref[hd, 2]).T.astype(bf16)
            kv_ref[ci, :, cs] = _dot(kz, v_ref[0, rs, cs])

    for hd, cs in enumerate(heads):
        state = jnp.zeros((RET_HEAD_DIM, RET_HEAD_DIM), f32)
        for ci in range(len(chunks)):
            st_ref[ci, :, cs] = state.astype(bf16)
            state = state * cst_ref[hd, 3] + kv_ref[ci, :, cs]

    for hd, cs in enumerate(heads):
        for ci, rs in enumerate(chunks):
            q_xi = (q_ref[0, rs, cs].astype(f32) * cst_ref[hd, 1]).astype(bf16)
            lhs = jnp.concatenate([inner_ref[rs, cs], q_xi], axis=1)
            rhs = jnp.concatenate([v_ref[0, rs, cs], st_ref[ci, :, cs]], axis=0)
            out = _dot(lhs, rhs)
            inv = lax.rsqrt(jnp.mean(out * out, axis=-1, keepdims=True) + EPS)
            gate = g_ref[0, rs, cs].astype(f32)
            o_ref[0, rs, cs] = (out * inv * gain_ref[:, cs] * gate).astype(bf16)


def _retention_constants():
    c = RET_CHUNK
    log_g = jnp.log(1.0 - 2.0 ** (-5.0 - jnp.arange(RET_HEADS, dtype=f32)))
    n = jnp.arange(c, dtype=f32)
    diff = n[:, None] - n[None, :]
    mask = jnp.where(diff[None] >= 0, jnp.exp(jnp.maximum(diff, 0.0)[None] * log_g[:, None, None]), 0.0)
    xi = jnp.exp((n[None, :] + 1.0) * log_g[:, None])
    zeta = jnp.exp((c - 1.0 - n[None, :]) * log_g[:, None])
    g_chunk = jnp.exp(c * log_g)
    bc = lambda rows: jnp.broadcast_to(rows[:, :, None], (RET_HEADS, c, c))
    return jnp.stack([mask, bc(xi), bc(zeta), jnp.broadcast_to(g_chunk[:, None, None], (RET_HEADS, c, c))], axis=1)


def _retention(rq, rk, rv, rg, out_gain):
    b, s, _ = rq.shape
    cst = _retention_constants()
    gain = out_gain.reshape(1, RET_WIDTH)
    seq = pl.BlockSpec((1, s, RET_WIDTH), lambda i: (i, 0, 0))
    return pl.pallas_call(
        _ret_kernel,
        grid=(b,),
        in_specs=[seq, seq, seq, seq,
                  pl.BlockSpec(cst.shape, lambda i: (0, 0, 0, 0)),
                  pl.BlockSpec(gain.shape, lambda i: (0, 0))],
        out_specs=seq,
        out_shape=jax.ShapeDtypeStruct((b, s, RET_WIDTH), bf16),
        scratch_shapes=[
            pltpu.VMEM((s, RET_WIDTH), bf16),
            pltpu.VMEM((s // RET_CHUNK, RET_HEAD_DIM, RET_WIDTH), f32),
            pltpu.VMEM((s // RET_CHUNK, RET_HEAD_DIM, RET_WIDTH), bf16),
        ],
        compiler_params=pltpu.CompilerParams(
            dimension_semantics=("arbitrary",), vmem_limit_bytes=VMEM_LIMIT),
        name="retention",
    )(rq, rk, rv, rg, cst, gain)


def _out_mlp_kernel(x_ref, mf_ref, mr_ref, mod_ref, wo_ref, w1_ref, w2_ref, o_ref):
    gate_m = mod_ref[0, 2:3, :]
    shift_f = mod_ref[0, 3:4, :]
    scale_f = mod_ref[0, 4:5, :]
    gate_f = mod_ref[0, 5:6, :]
    d_ff = w1_ref.shape[1]
    fc = min(FF_CHUNK, d_ff)
    tm = x_ref.shape[1]
    sub = min(TOKEN_TILE, tm)
    for r in range(tm // sub):
        rs = slice(r * sub, (r + 1) * sub)
        mixed = _dot(mf_ref[0, rs, :], wo_ref[0:FOX_WIDTH, :]) + _dot(mr_ref[0, rs, :], wo_ref[FOX_WIDTH:, :])
        x1 = x_ref[0, rs, :] + gate_m * mixed
        inv = lax.rsqrt(jnp.mean(x1 * x1, axis=-1, keepdims=True) + EPS)
        h = (x1 * inv * (1.0 + scale_f) + shift_f).astype(bf16)
        y = jnp.zeros(x1.shape, f32)
        for j in range(d_ff // fc):
            u = jnp.maximum(_dot(h, w1_ref[:, j * fc:(j + 1) * fc]), 0.0)
            y = y + _dot((u * u).astype(bf16), w2_ref[j * fc:(j + 1) * fc, :])
        o_ref[0, rs, :] = x1 + gate_f * y


def _out_mlp(x, mixed_fox, mixed_ret, mod, w_out, w_mlp_in, w_mlp_out):
    b, s, d = x.shape
    tm = min(2 * TOKEN_TILE, s)
    wo = w_out.astype(bf16)
    w1 = w_mlp_in.astype(bf16)
    w2 = w_mlp_out.astype(bf16)
    tok = lambda width: pl.BlockSpec((1, tm, width), lambda i, j: (i, j, 0))
    resident = lambda a: pl.BlockSpec(a.shape, lambda i, j: (0, 0), pipeline_mode=pl.Buffered(1))
    return pl.pallas_call(
        _out_mlp_kernel,
        grid=(b, s // tm),
        in_specs=[
            tok(d), tok(FOX_WIDTH), tok(RET_WIDTH),
            pl.BlockSpec((1, N_MOD, d), lambda i, j: (i, 0, 0)),
            resident(wo), resident(w1), resident(w2),
        ],
        out_specs=tok(d),
        out_shape=jax.ShapeDtypeStruct((b, s, d), f32),
        compiler_params=pltpu.CompilerParams(
            dimension_semantics=("arbitrary", "arbitrary"), vmem_limit_bytes=VMEM_LIMIT),
        name="out_mlp",
    )(x, mixed_fox, mixed_ret, mod, wo, w1, w2)


def kernel(x, c, w_ada, b_ada, w_in, b_forget, q_norm_gain, k_norm_gain, fox_out_gain, ret_out_gain,
           w_out, w_mlp_in, w_mlp_out):
    b, s, d = x.shape
    for l in range(w_ada.shape[0]):
        mod = _modulation(c, w_ada[l], b_ada[l]).reshape(b, N_MOD, d)
        fq, fk, fv, fog, log_f, rq, rk, rv, rg = _input_projection(
            x, mod, w_in[l], b_forget[l], q_norm_gain[l], k_norm_gain[l])
        mixed_fox = _fox_attention(fq, fk, fv, fog, log_f, fox_out_gain[l])
        mixed_ret = _retention(rq, rk, rv, rg, ret_out_gain[l])
        x = _out_mlp(x, mixed_fox, mixed_ret, mod, w_out[l], w_mlp_in[l], w_mlp_out[l])
    return x
```

```python
import numpy as np

import jax
import jax.numpy as jnp
from jax import lax
from jax.experimental import pallas as pl
from jax.experimental.pallas import tpu as pltpu

FOX_HEADS = 8
FOX_HEAD_DIM = 64
FOX_WIDTH = FOX_HEADS * FOX_HEAD_DIM
RET_HEADS = 4
RET_HEAD_DIM = 128
RET_WIDTH = RET_HEADS * RET_HEAD_DIM
RET_CHUNK = 128
ROPE_BASE = 10000.0
EPS = 1e-6
N_MOD = 6

LANES = 128
VMEM_LIMIT = 56 * 1024 * 1024
MASK_VALUE = -1e30
LOG2E = 1.4426950408889634

TOKEN_TILE = 512
ATTN_TILE = 256
FF_CHUNK = 1024

bf16 = jnp.bfloat16
f32 = jnp.float32


def _dot(a, b):
    return jnp.dot(a, b, preferred_element_type=f32)


def _dot_nt(a, b):
    return lax.dot_general(a, b, (((1,), (1,)), ((), ())), preferred_element_type=f32)


def _half_lane_mask():
    return lax.broadcasted_iota(jnp.int32, (1, LANES), 1) < FOX_HEAD_DIM


def _pair_sumsq(v, lo):
    sq = v * v
    s_lo = jnp.sum(jnp.where(lo, sq, 0.0), axis=-1, keepdims=True)
    s_hi = jnp.sum(jnp.where(lo, 0.0, sq), axis=-1, keepdims=True)
    return jnp.where(lo, s_lo, s_hi)


def _mod_kernel(c_ref, w_ref, b_ref, o_ref):
    c = c_ref[...]
    c_act = (c * jax.nn.sigmoid(c)).astype(bf16)
    o_ref[...] = _dot(c_act, w_ref[...].astype(bf16)) + b_ref[...]


def _modulation(c, w_ada, b_ada):
    b, d = c.shape
    n = w_ada.shape[1]
    return pl.pallas_call(
        _mod_kernel,
        grid=(n // d,),
        in_specs=[
            pl.BlockSpec((b, d), lambda j: (0, 0)),
            pl.BlockSpec((d, d), lambda j: (0, j)),
            pl.BlockSpec((1, d), lambda j: (0, j)),
        ],
        out_specs=pl.BlockSpec((b, d), lambda j: (0, j)),
        out_shape=jax.ShapeDtypeStruct((b, n), f32),
        compiler_params=pltpu.CompilerParams(
            dimension_semantics=("arbitrary",), vmem_limit_bytes=VMEM_LIMIT),
        name="adaln_mod",
    )(c, w_ada, b_ada.reshape(1, n))


def _inproj_kernel(x_ref, mod_ref, wf_ref, wg_ref, wr_ref, gq_ref, gk_ref, bf_ref, cos_ref, sin_ref,
                   fq_ref, fk_ref, fv_ref, fog_ref, lf_ref, rq_ref, rk_ref, rv_ref, rg_ref):
    lo = _half_lane_mask()
    w = FOX_WIDTH
    rw = RET_WIDTH
    tm = x_ref.shape[1]
    sub = min(TOKEN_TILE, tm)

    def qk_norm(p, gain_ref):
        outs = []
        for g in range(w // LANES):
            v = p[:, g * LANES:(g + 1) * LANES]
            ss = _pair_sumsq(v, lo)
            outs.append(v * lax.rsqrt(ss * (1.0 / FOX_HEAD_DIM) + EPS))
        return (jnp.concatenate(outs, axis=-1) * gain_ref[...]).astype(bf16)

    def rope(p, cos, sin, scale):
        outs = []
        for g in range(RET_HEADS):
            v = p[:, g * LANES:(g + 1) * LANES]
            outs.append(v * cos + pltpu.roll(v, RET_HEAD_DIM // 2, 1) * sin)
        r = jnp.concatenate(outs, axis=-1)
        return (r * scale if scale != 1.0 else r).astype(bf16)

    for r in range(tm // sub):
        rs = slice(r * sub, (r + 1) * sub)
        x = x_ref[0, rs, :]
        inv = lax.rsqrt(jnp.mean(x * x, axis=-1, keepdims=True) + EPS)
        h = (x * inv * (1.0 + mod_ref[0, 1:2, :]) + mod_ref[0, 0:1, :]).astype(bf16)

        def proj(w_ref, start, n):
            return _dot(h, w_ref[:, start:start + n])

        fq_ref[0, rs, :] = qk_norm(proj(wf_ref, 0, w), gq_ref)
        fk_ref[0, rs, :] = qk_norm(proj(wf_ref, w, w), gk_ref)
        fog_ref[0, rs, :] = jax.nn.sigmoid(proj(wf_ref, 3 * w, w)).astype(bf16)

        z = proj(wg_ref, 0, LANES) + bf_ref[...]
        log_f = jnp.minimum(z, 0.0) - jnp.log(1.0 + jnp.exp(-jnp.abs(z)))
        lf_ref[0, :, rs] = log_f.T[0:FOX_HEADS, :]

        cos = cos_ref[rs, :]
        sin = sin_ref[rs, :]
        rq_ref[0, rs, :] = rope(proj(wr_ref, 0, rw), cos, sin, 1.0)
        rk_ref[0, rs, :] = rope(proj(wr_ref, rw, rw), cos, sin, RET_HEAD_DIM ** -0.5)
        gate = proj(wr_ref, 3 * rw, rw)
        rg_ref[0, rs, :] = (gate * jax.nn.sigmoid(gate)).astype(bf16)
        rv_ref[0, rs, :] = proj(wr_ref, 2 * rw, rw).astype(bf16)
        fv_ref[0, rs, :] = proj(wf_ref, 2 * w, w).astype(bf16)


def _input_projection(x, mod, w_in, b_forget, q_gain, k_gain):
    b, s, d = x.shape
    tm = min(2 * TOKEN_TILE, s)
    o_ff = 4 * FOX_WIDTH
    o_r = o_ff + FOX_HEADS
    w_fox = w_in[:, :o_ff].astype(bf16)
    w_fg = jnp.pad(w_in[:, o_ff:o_r], ((0, 0), (0, LANES - FOX_HEADS))).astype(bf16)
    w_ret = w_in[:, o_r:].astype(bf16)
    bias_f = jnp.pad(b_forget, (0, LANES - FOX_HEADS)).reshape(1, LANES)
    gq = (jnp.tile(q_gain, FOX_HEADS) * (LOG2E * FOX_HEAD_DIM ** -0.5)).reshape(1, FOX_WIDTH)
    gk = jnp.tile(k_gain, FOX_HEADS).reshape(1, FOX_WIDTH)

    pos = np.arange(s, dtype=np.float32)
    inv_freq = np.float32(ROPE_BASE) ** (-np.arange(0, RET_HEAD_DIM, 2, dtype=np.float32) / np.float32(RET_HEAD_DIM))
    ang = pos[:, None] * inv_freq[None, :]
    cos_t = np.concatenate([np.cos(ang), np.cos(ang)], axis=-1).astype(np.float32)
    sin_t = np.concatenate([-np.sin(ang), np.sin(ang)], axis=-1).astype(np.float32)

    tok = lambda width: pl.BlockSpec((1, tm, width), lambda i, j: (i, j, 0))
    full = lambda a: pl.BlockSpec(a.shape, lambda i, j: (0,) * a.ndim, pipeline_mode=pl.Buffered(1))
    act = lambda width: jax.ShapeDtypeStruct((b, s, width), bf16)
    return pl.pallas_call(
        _inproj_kernel,
        grid=(b, s // tm),
        in_specs=[
            tok(d),
            pl.BlockSpec((1, N_MOD, d), lambda i, j: (i, 0, 0)),
            full(w_fox), full(w_fg), full(w_ret), full(gq), full(gk), full(bias_f),
            pl.BlockSpec((tm, LANES), lambda i, j: (j, 0)),
            pl.BlockSpec((tm, LANES), lambda i, j: (j, 0)),
        ],
        out_specs=[
            tok(FOX_WIDTH), tok(FOX_WIDTH), tok(FOX_WIDTH), tok(FOX_WIDTH),
            pl.BlockSpec((1, FOX_HEADS, tm), lambda i, j: (i, 0, j)),
            tok(RET_WIDTH), tok(RET_WIDTH), tok(RET_WIDTH), tok(RET_WIDTH),
        ],
        out_shape=[
            act(FOX_WIDTH), act(FOX_WIDTH), act(FOX_WIDTH), act(FOX_WIDTH),
            jax.ShapeDtypeStruct((b, FOX_HEADS, s), f32),
            act(RET_WIDTH), act(RET_WIDTH), act(RET_WIDTH), act(RET_WIDTH),
        ],
        compiler_params=pltpu.CompilerParams(
            dimension_semantics=("arbitrary", "arbitrary"), vmem_limit_bytes=VMEM_LIMIT),
        name="in_proj",
    )(x, mod, w_fox, w_fg, w_ret, gq, gk, bias_f, cos_t, sin_t)


def _lane_cumsum(x, out_ref):
    rows, s = x.shape
    r = lax.broadcasted_iota(jnp.int32, (LANES, LANES), 0)
    c = lax.broadcasted_iota(jnp.int32, (LANES, LANES), 1)
    tri = (r <= c).astype(bf16)
    hi = x.astype(bf16).astype(f32)
    rest = x - hi
    mid = rest.astype(bf16).astype(f32)
    low = (rest - mid).astype(bf16).astype(f32)
    n = s // LANES
    chunks = [slice(ch * LANES, (ch + 1) * LANES) for ch in range(n)]
    pieces = jnp.concatenate([g[:, sl] for g in (hi, mid, low) for sl in chunks], axis=0)
    rhs = jnp.concatenate([tri, jnp.ones((LANES, LANES), bf16)], axis=1)
    part = _dot(pieces.astype(bf16), rhs)
    offset = jnp.zeros((rows, LANES), f32)
    for ch, sl in enumerate(chunks):
        hi_c, mid_c, low_c = (part[(g * n + ch) * rows:(g * n + ch + 1) * rows] for g in range(3))
        both = hi_c + mid_c + low_c
        out_ref[:, sl] = both[:, 0:LANES] + offset
        offset = offset + both[:, LANES:]


def _fox_kernel(q_ref, k_ref, v_ref, lf_ref, og_ref, gain_ref, o_ref, cum_ref, s_ref):
    pair = pl.program_id(1)
    s = q_ref.shape[1]
    t = min(ATTN_TILE, s)

    @pl.when(pair == 0)
    def _():
        _lane_cumsum(lf_ref[0] * LOG2E, cum_ref)

    lo = _half_lane_mask()
    row = lax.broadcasted_iota(jnp.int32, (t, t), 0)
    col = lax.broadcasted_iota(jnp.int32, (t, t), 1)
    causal = row >= col
    cum_a = cum_ref[pl.ds(2 * pair, 1), :]
    cum_b = cum_ref[pl.ds(2 * pair + 1, 1), :]
    nq = s // t
    row_max, pv_acc = {}, {}

    def score_pass(qi):
        q = q_ref[0, qi * t:(qi + 1) * t, :]
        zero = jnp.zeros_like(q)
        q2 = jnp.concatenate([jnp.where(lo, q, zero), jnp.where(lo, zero, q)], axis=0)
        m_run = jnp.full((2 * t, LANES), MASK_VALUE, f32)
        for kb in range(qi + 1):
            ks = slice(kb * t, (kb + 1) * t)
            sc = _dot_nt(q2, k_ref[0, ks, :])
            sa = sc[0:t] - cum_a[:, ks]
            sb = sc[t:] - cum_b[:, ks]
            if kb == qi:
                sa = jnp.where(causal, sa, MASK_VALUE)
                sb = jnp.where(causal, sb, MASK_VALUE)
            sc = jnp.concatenate([sa, sb], axis=0)
            s_ref[qi % 2, :, ks] = sc
            for j in range(t // LANES):
                m_run = jnp.maximum(m_run, sc[:, j * LANES:(j + 1) * LANES])
            yield
        row_max[qi] = jnp.max(m_run, axis=-1, keepdims=True)

    def prob_pass(qi):
        m = row_max.pop(qi)
        acc = None
        for kb in range(qi + 1):
            ks = slice(kb * t, (kb + 1) * t)
            p = jnp.exp2(s_ref[qi % 2, :, ks] - m).astype(bf16)
            v1 = jnp.concatenate([v_ref[0, ks, :], jnp.ones((t, LANES), bf16)], axis=1)
            part = _dot(p, v1)
            acc = part if acc is None else acc + part
            yield
        pv_acc[qi] = acc

    def finish(qi):
        rows = slice(qi * t, (qi + 1) * t)
        pv = pv_acc.pop(qi)
        o2 = pv[:, 0:LANES] / pv[:, LANES:]
        out = jnp.where(lo, o2[0:t], o2[t:])
        ss = _pair_sumsq(out, lo)
        out = out * lax.rsqrt(ss * (1.0 / FOX_HEAD_DIM) + EPS) * gain_ref[0]
        o_ref[0, rows, :] = (out * og_ref[0, rows, :].astype(f32)).astype(bf16)

    def interleave(*passes):
        live = list(passes)
        while live:
            live = [g for g in live if next(g, StopIteration) is not StopIteration]

    interleave(score_pass(nq - 1))
    for qi in reversed(range(nq)):
        if qi + 1 < nq:
            finish(qi + 1)
        interleave(*([score_pass(qi - 1)] if qi >= 1 else []), prob_pass(qi))
    finish(0)


def _fox_attention(fq, fk, fv, fog, log_f, out_gain):
    b, s, _ = fq.shape
    t = min(ATTN_TILE, s)
    pairs = FOX_WIDTH // LANES
    gain = out_gain.reshape(pairs, 1, LANES)
    seq = pl.BlockSpec((1, s, LANES), lambda i, p: (i, 0, p))
    return pl.pallas_call(
        _fox_kernel,
        grid=(b, pairs),
        in_specs=[
            seq, seq, seq,
            pl.BlockSpec((1, FOX_HEADS, s), lambda i, p: (i, 0, 0)),
            seq,
            pl.BlockSpec((1, 1, LANES), lambda i, p: (p, 0, 0)),
        ],
        out_specs=seq,
        out_shape=jax.ShapeDtypeStruct((b, s, FOX_WIDTH), bf16),
        scratch_shapes=[
            pltpu.VMEM((FOX_HEADS, s), f32),
            pltpu.VMEM((2, 2 * t, s), f32),
        ],
        compiler_params=pltpu.CompilerParams(
            dimension_semantics=("arbitrary", "arbitrary"), vmem_limit_bytes=VMEM_LIMIT),
        name="fox_attention",
    )(fq, fk, fv, log_f, fog, gain)


def _ret_kernel(q_ref, k_ref, v_ref, g_ref, cst_ref, gain_ref, o_ref, inner_ref, kv_ref, st_ref):
    s = q_ref.shape[1]
    c = RET_CHUNK
    heads = [slice(hd * RET_HEAD_DIM, (hd + 1) * RET_HEAD_DIM) for hd in range(RET_HEADS)]
    chunks = [slice(ci * c, (ci + 1) * c) for ci in range(s // c)]

    for hd, cs in enumerate(heads):
        for ci, rs in enumerate(chunks):
            k = k_ref[0, rs, cs]
            inner_ref[rs, cs] = (_dot_nt(q_ref[0, rs, cs], k) * cst_ref[hd, 0]).astype(bf16)
            kz = (k.astype(f32) * cst_ref[hd, 2]).T.astype(bf16)
            kv_ref[ci, :, cs] = _dot(kz, v_ref[0, rs, cs])

    for hd, cs in enumerate(heads):
        state = jnp.zeros((RET_HEAD_DIM, RET_HEAD_DIM), f32)
        for ci in range(len(chunks)):
            st_ref[ci, :, cs] = state.astype(bf16)
            state = state * cst_ref[hd, 3] + kv_ref[ci, :, cs]

    for hd, cs in enumerate(heads):
        for ci, rs in enumerate(chunks):
            q_xi = (q_ref[0, rs, cs].astype(f32) * cst_ref[hd, 1]).astype(bf16)
            lhs = jnp.concatenate([inner_ref[rs, cs], q_xi], axis=1)
            rhs = jnp.concatenate([v_ref[0, rs, cs], st_ref[ci, :, cs]], axis=0)
            out = _dot(lhs, rhs)
            inv = lax.rsqrt(jnp.mean(out * out, axis=-1, keepdims=True) + EPS)
            gate = g_ref[0, rs, cs].astype(f32)
            o_ref[0, rs, cs] = (out * inv * gain_ref[:, cs] * gate).astype(bf16)


def _retention_constants():
    c = RET_CHUNK
    f = np.float32
    log_g = np.log(f(1.0) - f(2.0) ** (f(-5.0) - np.arange(RET_HEADS, dtype=f)))
    n = np.arange(c, dtype=f)
    diff = n[:, None] - n[None, :]
    mask = np.where(diff[None] >= 0, np.exp(np.maximum(diff, f(0.0))[None] * log_g[:, None, None]), f(0.0))
    xi = np.exp((n[None, :] + f(1.0)) * log_g[:, None])
    zeta = np.exp((f(c) - f(1.0) - n[None, :]) * log_g[:, None])
    g_chunk = np.exp(f(c) * log_g)
    bc = lambda rows: np.broadcast_to(rows[:, :, None], (RET_HEADS, c, c))
    return np.stack([mask, bc(xi), bc(zeta), np.broadcast_to(g_chunk[:, None, None], (RET_HEADS, c, c))], axis=1).astype(f)


def _retention(rq, rk, rv, rg, out_gain):
    b, s, _ = rq.shape
    cst = _retention_constants()
    gain = out_gain.reshape(1, RET_WIDTH)
    seq = pl.BlockSpec((1, s, RET_WIDTH), lambda i: (i, 0, 0))
    return pl.pallas_call(
        _ret_kernel,
        grid=(b,),
        in_specs=[seq, seq, seq, seq,
                  pl.BlockSpec(cst.shape, lambda i: (0, 0, 0, 0)),
                  pl.BlockSpec(gain.shape, lambda i: (0, 0))],
        out_specs=seq,
        out_shape=jax.ShapeDtypeStruct((b, s, RET_WIDTH), bf16),
        scratch_shapes=[
            pltpu.VMEM((s, RET_WIDTH), bf16),
            pltpu.VMEM((s // RET_CHUNK, RET_HEAD_DIM, RET_WIDTH), f32),
            pltpu.VMEM((s // RET_CHUNK, RET_HEAD_DIM, RET_WIDTH), bf16),
        ],
        compiler_params=pltpu.CompilerParams(
            dimension_semantics=("arbitrary",), vmem_limit_bytes=VMEM_LIMIT),
        name="retention",
    )(rq, rk, rv, rg, cst, gain)


def _out_mlp_kernel(x_ref, mf_ref, mr_ref, mod_ref, wo_ref, w1_ref, w2_ref, o_ref):
    gate_m = mod_ref[0, 2:3, :]
    shift_f = mod_ref[0, 3:4, :]
    scale_f = mod_ref[0, 4:5, :]
    gate_f = mod_ref[0, 5:6, :]
    d_ff = w1_ref.shape[1]
    fc = min(FF_CHUNK, d_ff)
    tm = x_ref.shape[1]
    sub = min(TOKEN_TILE, tm)
    for r in range(tm // sub):
        rs = slice(r * sub, (r + 1) * sub)
        mixed = _dot(mf_ref[0, rs, :], wo_ref[0:FOX_WIDTH, :]) + _dot(mr_ref[0, rs, :], wo_ref[FOX_WIDTH:, :])
        x1 = x_ref[0, rs, :] + gate_m * mixed
        inv = lax.rsqrt(jnp.mean(x1 * x1, axis=-1, keepdims=True) + EPS)
        h = (x1 * inv * (1.0 + scale_f) + shift_f).astype(bf16)
        y = jnp.zeros(x1.shape, f32)
        for j in range(d_ff // fc):
            u = jnp.maximum(_dot(h, w1_ref[:, j * fc:(j + 1) * fc]), 0.0)
            y = y + _dot((u * u).astype(bf16), w2_ref[j * fc:(j + 1) * fc, :])
        o_ref[0, rs, :] = x1 + gate_f * y


def _out_mlp(x, mixed_fox, mixed_ret, mod, w_out, w_mlp_in, w_mlp_out):
    b, s, d = x.shape
    tm = min(2 * TOKEN_TILE, s)
    wo = w_out.astype(bf16)
    w1 = w_mlp_in.astype(bf16)
    w2 = w_mlp_out.astype(bf16)
    tok = lambda width: pl.BlockSpec((1, tm, width), lambda i, j: (i, j, 0))
    resident = lambda a: pl.BlockSpec(a.shape, lambda i, j: (0, 0), pipeline_mode=pl.Buffered(1))
    return pl.pallas_call(
        _out_mlp_kernel,
        grid=(b, s // tm),
        in_specs=[
            tok(d), tok(FOX_WIDTH), tok(RET_WIDTH),
            pl.BlockSpec((1, N_MOD, d), lambda i, j: (i, 0, 0)),
            resident(wo), resident(w1), resident(w2),
        ],
        out_specs=tok(d),
        out_shape=jax.ShapeDtypeStruct((b, s, d), f32),
        compiler_params=pltpu.CompilerParams(
            dimension_semantics=("arbitrary", "arbitrary"), vmem_limit_bytes=VMEM_LIMIT),
        name="out_mlp",
    )(x, mixed_fox, mixed_ret, mod, wo, w1, w2)


def kernel(x, c, w_ada, b_ada, w_in, b_forget, q_norm_gain, k_norm_gain, fox_out_gain, ret_out_gain,
           w_out, w_mlp_in, w_mlp_out):
    b, s, d = x.shape
    for l in range(w_ada.shape[0]):
        mod = _modulation(c, w_ada[l], b_ada[l]).reshape(b, N_MOD, d)
        fq, fk, fv, fog, log_f, rq, rk, rv, rg = _input_projection(
            x, mod, w_in[l], b_forget[l], q_norm_gain[l], k_norm_gain[l])
        mixed_fox = _fox_attention(fq, fk, fv, fog, log_f, fox_out_gain[l])
        mixed_ret = _retention(rq, rk, rv, rg, ret_out_gain[l])
        x = _out_mlp(x, mixed_fox, mixed_ret, mod, w_out[l], w_mlp_in[l], w_mlp_out[l])
    return x
```

```python
import numpy as np

import jax
import jax.numpy as jnp
from jax import lax
from jax.experimental import pallas as pl
from jax.experimental.pallas import tpu as pltpu

FOX_HEADS = 8
FOX_HEAD_DIM = 64
FOX_WIDTH = FOX_HEADS * FOX_HEAD_DIM
RET_HEADS = 4
RET_HEAD_DIM = 128
RET_WIDTH = RET_HEADS * RET_HEAD_DIM
RET_CHUNK = 128
ROPE_BASE = 10000.0
EPS = 1e-6
N_MOD = 6

LANES = 128
VMEM_LIMIT = 56 * 1024 * 1024
MASK_VALUE = -1e30
LOG2E = 1.4426950408889634

TOKEN_TILE = 512
ATTN_TILE = 256
FF_CHUNK = 1024

bf16 = jnp.bfloat16
f32 = jnp.float32


def _dot(a, b):
    return jnp.dot(a, b, preferred_element_type=f32)


def _dot_nt(a, b):
    return lax.dot_general(a, b, (((1,), (1,)), ((), ())), preferred_element_type=f32)


def _half_lane_mask():
    return lax.broadcasted_iota(jnp.int32, (1, LANES), 1) < FOX_HEAD_DIM


def _pair_sumsq(v, lo):
    sq = v * v
    s_lo = jnp.sum(jnp.where(lo, sq, 0.0), axis=-1, keepdims=True)
    s_hi = jnp.sum(jnp.where(lo, 0.0, sq), axis=-1, keepdims=True)
    return jnp.where(lo, s_lo, s_hi)


def _mod_kernel(c_ref, w_ref, b_ref, o_ref):
    c = c_ref[...]
    c_act = (c * jax.nn.sigmoid(c)).astype(bf16)
    o_ref[...] = _dot(c_act, w_ref[...].astype(bf16)) + b_ref[...]


def _modulation(c, w_ada, b_ada):
    b, d = c.shape
    n = w_ada.shape[1]
    return pl.pallas_call(
        _mod_kernel,
        grid=(n // d,),
        in_specs=[
            pl.BlockSpec((b, d), lambda j: (0, 0)),
            pl.BlockSpec((d, d), lambda j: (0, j)),
            pl.BlockSpec((1, d), lambda j: (0, j)),
        ],
        out_specs=pl.BlockSpec((b, d), lambda j: (0, j)),
        out_shape=jax.ShapeDtypeStruct((b, n), f32),
        compiler_params=pltpu.CompilerParams(
            dimension_semantics=("arbitrary",), vmem_limit_bytes=VMEM_LIMIT),
        name="adaln_mod",
    )(c, w_ada, b_ada.reshape(1, n))


def _inproj_kernel(x_ref, mod_ref, wf_ref, wg_ref, wr_ref, gq_ref, gk_ref, bf_ref, cos_ref, sin_ref,
                   fq_ref, fk_ref, fv_ref, fog_ref, lf_ref, rq_ref, rk_ref, rv_ref, rg_ref):
    lo = _half_lane_mask()
    w = FOX_WIDTH
    rw = RET_WIDTH
    tm = x_ref.shape[1]
    sub = min(TOKEN_TILE, tm)

    def qk_norm(p, gain_ref):
        outs = []
        for g in range(w // LANES):
            v = p[:, g * LANES:(g + 1) * LANES]
            ss = _pair_sumsq(v, lo)
            outs.append(v * lax.rsqrt(ss * (1.0 / FOX_HEAD_DIM) + EPS))
        return (jnp.concatenate(outs, axis=-1) * gain_ref[...]).astype(bf16)

    def rope(p, cos, sin, scale):
        outs = []
        for g in range(RET_HEADS):
            v = p[:, g * LANES:(g + 1) * LANES]
            outs.append(v * cos + pltpu.roll(v, RET_HEAD_DIM // 2, 1) * sin)
        r = jnp.concatenate(outs, axis=-1)
        return (r * scale if scale != 1.0 else r).astype(bf16)

    for r in range(tm // sub):
        rs = slice(r * sub, (r + 1) * sub)
        x = x_ref[0, rs, :]
        inv = lax.rsqrt(jnp.mean(x * x, axis=-1, keepdims=True) + EPS)
        h = (x * inv * (1.0 + mod_ref[0, 1:2, :]) + mod_ref[0, 0:1, :]).astype(bf16)

        def proj(w_ref, start, n):
            return _dot(h, w_ref[:, start:start + n])

        fq_ref[0, rs, :] = qk_norm(proj(wf_ref, 0, w), gq_ref)
        fk_ref[0, rs, :] = qk_norm(proj(wf_ref, w, w), gk_ref)
        fog_ref[0, rs, :] = jax.nn.sigmoid(proj(wf_ref, 3 * w, w)).astype(bf16)

        z = proj(wg_ref, 0, LANES) + bf_ref[...]
        log_f = jnp.minimum(z, 0.0) - jnp.log(1.0 + jnp.exp(-jnp.abs(z)))
        lf_ref[0, :, rs] = log_f.T[0:FOX_HEADS, :]

        cos = cos_ref[rs, :]
        sin = sin_ref[rs, :]
        rq_ref[0, rs, :] = rope(proj(wr_ref, 0, rw), cos, sin, 1.0)
        rk_ref[0, rs, :] = rope(proj(wr_ref, rw, rw), cos, sin, RET_HEAD_DIM ** -0.5)
        gate = proj(wr_ref, 3 * rw, rw)
        rg_ref[0, rs, :] = (gate * jax.nn.sigmoid(gate)).astype(bf16)
        rv_ref[0, rs, :] = proj(wr_ref, 2 * rw, rw).astype(bf16)
        fv_ref[0, rs, :] = proj(wf_ref, 2 * w, w).astype(bf16)


def _input_projection(x, mod, w_in, b_forget, q_gain, k_gain):
    b, s, d = x.shape
    tm = min(2 * TOKEN_TILE, s)
    o_ff = 4 * FOX_WIDTH
    o_r = o_ff + FOX_HEADS
    w_fox = w_in[:, :o_ff].astype(bf16)
    w_fg = jnp.pad(w_in[:, o_ff:o_r], ((0, 0), (0, LANES - FOX_HEADS))).astype(bf16)
    w_ret = w_in[:, o_r:].astype(bf16)
    bias_f = jnp.pad(b_forget, (0, LANES - FOX_HEADS)).reshape(1, LANES)
    gq = (jnp.tile(q_gain, FOX_HEADS) * (LOG2E * FOX_HEAD_DIM ** -0.5)).reshape(1, FOX_WIDTH)
    gk = jnp.tile(k_gain, FOX_HEADS).reshape(1, FOX_WIDTH)

    pos = np.arange(s, dtype=np.float32)
    inv_freq = np.float32(ROPE_BASE) ** (-np.arange(0, RET_HEAD_DIM, 2, dtype=np.float32) / np.float32(RET_HEAD_DIM))
    ang = pos[:, None] * inv_freq[None, :]
    cos_t = np.concatenate([np.cos(ang), np.cos(ang)], axis=-1).astype(np.float32)
    sin_t = np.concatenate([-np.sin(ang), np.sin(ang)], axis=-1).astype(np.float32)

    tok = lambda width: pl.BlockSpec((1, tm, width), lambda i, j: (i, j, 0))
    full = lambda a: pl.BlockSpec(a.shape, lambda i, j: (0,) * a.ndim, pipeline_mode=pl.Buffered(1))
    act = lambda width: jax.ShapeDtypeStruct((b, s, width), bf16)
    return pl.pallas_call(
        _inproj_kernel,
        grid=(b, s // tm),
        in_specs=[
            tok(d),
            pl.BlockSpec((1, N_MOD, d), lambda i, j: (i, 0, 0)),
            full(w_fox), full(w_fg), full(w_ret), full(gq), full(gk), full(bias_f),
            pl.BlockSpec((tm, LANES), lambda i, j: (j, 0)),
            pl.BlockSpec((tm, LANES), lambda i, j: (j, 0)),
        ],
        out_specs=[
            tok(FOX_WIDTH), tok(FOX_WIDTH), tok(FOX_WIDTH), tok(FOX_WIDTH),
            pl.BlockSpec((1, FOX_HEADS, tm), lambda i, j: (i, 0, j)),
            tok(RET_WIDTH), tok(RET_WIDTH), tok(RET_WIDTH), tok(RET_WIDTH),
        ],
        out_shape=[
            act(FOX_WIDTH), act(FOX_WIDTH), act(FOX_WIDTH), act(FOX_WIDTH),
            jax.ShapeDtypeStruct((b, FOX_HEADS, s), f32),
            act(RET_WIDTH), act(RET_WIDTH), act(RET_WIDTH), act(RET_WIDTH),
        ],
        compiler_params=pltpu.CompilerParams(
            dimension_semantics=("arbitrary", "arbitrary"), vmem_limit_bytes=VMEM_LIMIT),
        name="in_proj",
    )(x, mod, w_fox, w_fg, w_ret, gq, gk, bias_f, cos_t, sin_t)


def _lane_cumsum(x, out_ref):
    rows, s = x.shape
    r = lax.broadcasted_iota(jnp.int32, (LANES, LANES), 0)
    c = lax.broadcasted_iota(jnp.int32, (LANES, LANES), 1)
    tri = (r <= c).astype(bf16)
    hi = x.astype(bf16).astype(f32)
    rest = x - hi
    mid = rest.astype(bf16).astype(f32)
    low = (rest - mid).astype(bf16).astype(f32)
    n = s // LANES
    chunks = [slice(ch * LANES, (ch + 1) * LANES) for ch in range(n)]
    pieces = jnp.concatenate([g[:, sl] for g in (hi, mid, low) for sl in chunks], axis=0)
    rhs = jnp.concatenate([tri, jnp.ones((LANES, LANES), bf16)], axis=1)
    part = _dot(pieces.astype(bf16), rhs)
    offset = jnp.zeros((rows, LANES), f32)
    for ch, sl in enumerate(chunks):
        hi_c, mid_c, low_c = (part[(g * n + ch) * rows:(g * n + ch + 1) * rows] for g in range(3))
        both = hi_c + mid_c + low_c
        out_ref[:, sl] = both[:, 0:LANES] + offset
        offset = offset + both[:, LANES:]


def _fox_kernel(q_ref, k_ref, v_ref, lf_ref, og_ref, gain_ref, o_ref, cum_ref, kx_ref, s_ref):
    pair = pl.program_id(1)
    s = q_ref.shape[1]
    t = min(ATTN_TILE, s)

    @pl.when(pair == 0)
    def _():
        _lane_cumsum(lf_ref[0] * LOG2E, cum_ref)

    lo = _half_lane_mask()
    row = lax.broadcasted_iota(jnp.int32, (t, t), 0)
    col = lax.broadcasted_iota(jnp.int32, (t, t), 1)
    causal = row >= col
    def bf16_pieces(v):
        hi = v.astype(bf16).astype(f32)
        mid = (v - hi).astype(bf16).astype(f32)
        return [hi, mid, (v - hi - mid).astype(bf16).astype(f32)]

    pieces = (bf16_pieces(-cum_ref[pl.ds(2 * pair, 1), :])
              + bf16_pieces(-cum_ref[pl.ds(2 * pair + 1, 1), :]))
    sub = lax.broadcasted_iota(jnp.int32, (8, 1), 0)
    decay_t = jnp.zeros((8, s), f32)
    for r, piece in enumerate(pieces):
        decay_t = jnp.where(sub == r, piece, decay_t)
    decay_t = jnp.concatenate([decay_t, jnp.zeros((LANES - 8, s), f32)], axis=0)
    kx_ref[:, 0:LANES] = k_ref[0]
    kx_ref[:, LANES:] = decay_t.T.astype(bf16)
    lane = lax.broadcasted_iota(jnp.int32, (1, LANES), 1)
    ones_a = jnp.where(lane < 3, 1.0, 0.0).astype(bf16)
    ones_b = jnp.where((lane >= 3) & (lane < 6), 1.0, 0.0).astype(bf16)
    nq = s // t
    row_max, pv_acc = {}, {}

    def score_pass(qi):
        q = q_ref[0, qi * t:(qi + 1) * t, :]
        zero = jnp.zeros_like(q)
        q2 = jnp.concatenate(
            [jnp.concatenate([jnp.where(lo, q, zero), jnp.broadcast_to(ones_a, q.shape)], axis=1),
             jnp.concatenate([jnp.where(lo, zero, q), jnp.broadcast_to(ones_b, q.shape)], axis=1)], axis=0)
        m_run = jnp.full((2 * t, LANES), MASK_VALUE, f32)
        for kb in range(qi + 1):
            ks = slice(kb * t, (kb + 1) * t)
            sc = _dot_nt(q2, kx_ref[ks, :])
            if kb == qi:
                sc = jnp.concatenate([jnp.where(causal, sc[0:t], MASK_VALUE),
                                      jnp.where(causal, sc[t:], MASK_VALUE)], axis=0)
            s_ref[qi % 2, :, ks] = sc
            for j in range(t // LANES):
                m_run = jnp.maximum(m_run, sc[:, j * LANES:(j + 1) * LANES])
            yield
        row_max[qi] = jnp.max(m_run, axis=-1, keepdims=True)

    def prob_pass(qi):
        m = row_max.pop(qi)
        acc = None
        for kb in range(qi + 1):
            ks = slice(kb * t, (kb + 1) * t)
            p = jnp.exp2(s_ref[qi % 2, :, ks] - m).astype(bf16)
            v1 = jnp.concatenate([v_ref[0, ks, :], jnp.ones((t, LANES), bf16)], axis=1)
            part = _dot(p, v1)
            acc = part if acc is None else acc + part
            yield
        pv_acc[qi] = acc

    def finish(qi):
        rows = slice(qi * t, (qi + 1) * t)
        pv = pv_acc.pop(qi)
        o2 = pv[:, 0:LANES] / pv[:, LANES:]
        out = jnp.where(lo, o2[0:t], o2[t:])
        ss = _pair_sumsq(out, lo)
        out = out * lax.rsqrt(ss * (1.0 / FOX_HEAD_DIM) + EPS) * gain_ref[0]
        o_ref[0, rows, :] = (out * og_ref[0, rows, :].astype(f32)).astype(bf16)

    def interleave(*passes):
        live = list(passes)
        while live:
            live = [g for g in live if next(g, StopIteration) is not StopIteration]

    interleave(score_pass(nq - 1))
    for qi in reversed(range(nq)):
        if qi + 1 < nq:
            finish(qi + 1)
        interleave(*([score_pass(qi - 1)] if qi >= 1 else []), prob_pass(qi))
    finish(0)


def _fox_attention(fq, fk, fv, fog, log_f, out_gain):
    b, s, _ = fq.shape
    t = min(ATTN_TILE, s)
    pairs = FOX_WIDTH // LANES
    gain = out_gain.reshape(pairs, 1, LANES)
    seq = pl.BlockSpec((1, s, LANES), lambda i, p: (i, 0, p))
    return pl.pallas_call(
        _fox_kernel,
        grid=(b, pairs),
        in_specs=[
            seq, seq, seq,
            pl.BlockSpec((1, FOX_HEADS, s), lambda i, p: (i, 0, 0)),
            seq,
            pl.BlockSpec((1, 1, LANES), lambda i, p: (p, 0, 0)),
        ],
        out_specs=seq,
        out_shape=jax.ShapeDtypeStruct((b, s, FOX_WIDTH), bf16),
        scratch_shapes=[
            pltpu.VMEM((FOX_HEADS, s), f32),
            pltpu.VMEM((s, 2 * LANES), bf16),
            pltpu.VMEM((2, 2 * t, s), f32),
        ],
        compiler_params=pltpu.CompilerParams(
            dimension_semantics=("arbitrary", "arbitrary"), vmem_limit_bytes=VMEM_LIMIT),
        name="fox_attention",
    )(fq, fk, fv, log_f, fog, gain)


def _ret_kernel(q_ref, k_ref, v_ref, g_ref, cst_ref, gain_ref, o_ref, inner_ref, kv_ref, st_ref):
    s = q_ref.shape[1]
    c = RET_CHUNK
    heads = [slice(hd * RET_HEAD_DIM, (hd + 1) * RET_HEAD_DIM) for hd in range(RET_HEADS)]
    chunks = [slice(ci * c, (ci + 1) * c) for ci in range(s // c)]

    for hd, cs in enumerate(heads):
        for ci, rs in enumerate(chunks):
            k = k_ref[0, rs, cs]
            inner_ref[rs, cs] = (_dot_nt(q_ref[0, rs, cs], k) * cst_ref[hd, 0]).astype(bf16)
            kz = (k.astype(f32) * cst_ref[hd, 2]).T.astype(bf16)
            kv_ref[ci, :, cs] = _dot(kz, v_ref[0, rs, cs])

    for hd, cs in enumerate(heads):
        state = jnp.zeros((RET_HEAD_DIM, RET_HEAD_DIM), f32)
        for ci in range(len(chunks)):
            st_ref[ci, :, cs] = state.astype(bf16)
            state = state * cst_ref[hd, 3] + kv_ref[ci, :, cs]

    for hd, cs in enumerate(heads):
        for ci, rs in enumerate(chunks):
            q_xi = (q_ref[0, rs, cs].astype(f32) * cst_ref[hd, 1]).astype(bf16)
            lhs = jnp.concatenate([inner_ref[rs, cs], q_xi], axis=1)
            rhs = jnp.concatenate([v_ref[0, rs, cs], st_ref[ci, :, cs]], axis=0)
            out = _dot(lhs, rhs)
            inv = lax.rsqrt(jnp.mean(out * out, axis=-1, keepdims=True) + EPS)
            gate = g_ref[0, rs, cs].astype(f32)
            o_ref[0, rs, cs] = (out * inv * gain_ref[:, cs] * gate).astype(bf16)


def _retention_constants():
    c = RET_CHUNK
    f = np.float32
    log_g = np.log(f(1.0) - f(2.0) ** (f(-5.0) - np.arange(RET_HEADS, dtype=f)))
    n = np.arange(c, dtype=f)
    diff = n[:, None] - n[None, :]
    mask = np.where(diff[None] >= 0, np.exp(np.maximum(diff, f(0.0))[None] * log_g[:, None, None]), f(0.0))
    xi = np.exp((n[None, :] + f(1.0)) * log_g[:, None])
    zeta = np.exp((f(c) - f(1.0) - n[None, :]) * log_g[:, None])
    g_chunk = np.exp(f(c) * log_g)
    bc = lambda rows: np.broadcast_to(rows[:, :, None], (RET_HEADS, c, c))
    return np.stack([mask, bc(xi), bc(zeta), np.broadcast_to(g_chunk[:, None, None], (RET_HEADS, c, c))], axis=1).astype(f)


def _retention(rq, rk, rv, rg, out_gain):
    b, s, _ = rq.shape
    cst = _retention_constants()
    gain = out_gain.reshape(1, RET_WIDTH)
    seq = pl.BlockSpec((1, s, RET_WIDTH), lambda i: (i, 0, 0))
    return pl.pallas_call(
        _ret_kernel,
        grid=(b,),
        in_specs=[seq, seq, seq, seq,
                  pl.BlockSpec(cst.shape, lambda i: (0, 0, 0, 0)),
                  pl.BlockSpec(gain.shape, lambda i: (0, 0))],
        out_specs=seq,
        out_shape=jax.ShapeDtypeStruct((b, s, RET_WIDTH), bf16),
        scratch_shapes=[
            pltpu.VMEM((s, RET_WIDTH), bf16),
            pltpu.VMEM((s // RET_CHUNK, RET_HEAD_DIM, RET_WIDTH), f32),
            pltpu.VMEM((s // RET_CHUNK, RET_HEAD_DIM, RET_WIDTH), bf16),
        ],
        compiler_params=pltpu.CompilerParams(
            dimension_semantics=("arbitrary",), vmem_limit_bytes=VMEM_LIMIT),
        name="retention",
    )(rq, rk, rv, rg, cst, gain)


def _out_mlp_kernel(x_ref, mf_ref, mr_ref, mod_ref, wo_ref, w1_ref, w2_ref, o_ref):
    gate_m = mod_ref[0, 2:3, :]
    shift_f = mod_ref[0, 3:4, :]
    scale_f = mod_ref[0, 4:5, :]
    gate_f = mod_ref[0, 5:6, :]
    d_ff = w1_ref.shape[1]
    fc = min(FF_CHUNK, d_ff)
    tm = x_ref.shape[1]
    sub = min(TOKEN_TILE, tm)
    for r in range(tm // sub):
        rs = slice(r * sub, (r + 1) * sub)
        mixed = _dot(mf_ref[0, rs, :], wo_ref[0:FOX_WIDTH, :]) + _dot(mr_ref[0, rs, :], wo_ref[FOX_WIDTH:, :])
        x1 = x_ref[0, rs, :] + gate_m * mixed
        inv = lax.rsqrt(jnp.mean(x1 * x1, axis=-1, keepdims=True) + EPS)
        h = (x1 * inv * (1.0 + scale_f) + shift_f).astype(bf16)
        y = jnp.zeros(x1.shape, f32)
        for j in range(d_ff // fc):
            u = jnp.maximum(_dot(h, w1_ref[:, j * fc:(j + 1) * fc]), 0.0)
            y = y + _dot((u * u).astype(bf16), w2_ref[j * fc:(j + 1) * fc, :])
        o_ref[0, rs, :] = x1 + gate_f * y


def _out_mlp(x, mixed_fox, mixed_ret, mod, w_out, w_mlp_in, w_mlp_out):
    b, s, d = x.shape
    tm = min(2 * TOKEN_TILE, s)
    wo = w_out.astype(bf16)
    w1 = w_mlp_in.astype(bf16)
    w2 = w_mlp_out.astype(bf16)
    tok = lambda width: pl.BlockSpec((1, tm, width), lambda i, j: (i, j, 0))
    resident = lambda a: pl.BlockSpec(a.shape, lambda i, j: (0, 0), pipeline_mode=pl.Buffered(1))
    return pl.pallas_call(
        _out_mlp_kernel,
        grid=(b, s // tm),
        in_specs=[
            tok(d), tok(FOX_WIDTH), tok(RET_WIDTH),
            pl.BlockSpec((1, N_MOD, d), lambda i, j: (i, 0, 0)),
            resident(wo), resident(w1), resident(w2),
        ],
        out_specs=tok(d),
        out_shape=jax.ShapeDtypeStruct((b, s, d), f32),
        compiler_params=pltpu.CompilerParams(
            dimension_semantics=("arbitrary", "arbitrary"), vmem_limit_bytes=VMEM_LIMIT),
        name="out_mlp",
    )(x, mixed_fox, mixed_ret, mod, wo, w1, w2)


def kernel(x, c, w_ada, b_ada, w_in, b_forget, q_norm_gain, k_norm_gain, fox_out_gain, ret_out_gain,
           w_out, w_mlp_in, w_mlp_out):
    b, s, d = x.shape
    for l in range(w_ada.shape[0]):
        mod = _modulation(c, w_ada[l], b_ada[l]).reshape(b, N_MOD, d)
        fq, fk, fv, fog, log_f, rq, rk, rv, rg = _input_projection(
            x, mod, w_in[l], b_forget[l], q_norm_gain[l], k_norm_gain[l])
        mixed_fox = _fox_attention(fq, fk, fv, fog, log_f, fox_out_gain[l])
        mixed_ret = _retention(rq, rk, rv, rg, ret_out_gain[l])
        x = _out_mlp(x, mixed_fox, mixed_ret, mod, w_out[l], w_mlp_in[l], w_mlp_out[l])
    return x
```

```python
import numpy as np

import jax
import jax.numpy as jnp
from jax import lax
from jax.experimental import pallas as pl
from jax.experimental.pallas import tpu as pltpu

FOX_HEADS = 8
FOX_HEAD_DIM = 64
FOX_WIDTH = FOX_HEADS * FOX_HEAD_DIM
RET_HEADS = 4
RET_HEAD_DIM = 128
RET_WIDTH = RET_HEADS * RET_HEAD_DIM
RET_CHUNK = 128
ROPE_BASE = 10000.0
EPS = 1e-6
N_MOD = 6

LANES = 128
VMEM_LIMIT = 56 * 1024 * 1024
MASK_VALUE = -1e30
LOG2E = 1.4426950408889634

TOKEN_TILE = 512
ATTN_TILE = 256
FF_CHUNK = 1024

bf16 = jnp.bfloat16
f32 = jnp.float32


def _dot(a, b):
    return jnp.dot(a, b, preferred_element_type=f32)


def _dot_nt(a, b):
    return lax.dot_general(a, b, (((1,), (1,)), ((), ())), preferred_element_type=f32)


def _half_lane_mask():
    return lax.broadcasted_iota(jnp.int32, (1, LANES), 1) < FOX_HEAD_DIM


def _pair_sumsq(v, lo):
    sq = v * v
    s_lo = jnp.sum(jnp.where(lo, sq, 0.0), axis=-1, keepdims=True)
    s_hi = jnp.sum(jnp.where(lo, 0.0, sq), axis=-1, keepdims=True)
    return jnp.where(lo, s_lo, s_hi)


def _mod_kernel(c_ref, w_ref, b_ref, o_ref):
    c = c_ref[...]
    c_act = (c * jax.nn.sigmoid(c)).astype(bf16)
    o_ref[...] = _dot(c_act, w_ref[...].astype(bf16)) + b_ref[...]


def _modulation(c, w_ada, b_ada):
    b, d = c.shape
    n = w_ada.shape[1]
    return pl.pallas_call(
        _mod_kernel,
        grid=(n // d,),
        in_specs=[
            pl.BlockSpec((b, d), lambda j: (0, 0)),
            pl.BlockSpec((d, d), lambda j: (0, j)),
            pl.BlockSpec((1, d), lambda j: (0, j)),
        ],
        out_specs=pl.BlockSpec((b, d), lambda j: (0, j)),
        out_shape=jax.ShapeDtypeStruct((b, n), f32),
        compiler_params=pltpu.CompilerParams(
            dimension_semantics=("arbitrary",), vmem_limit_bytes=VMEM_LIMIT),
        name="adaln_mod",
    )(c, w_ada, b_ada.reshape(1, n))


def _inproj_kernel(x_ref, mod_ref, wf_ref, wg_ref, wr_ref, gq_ref, gk_ref, bf_ref, cos_ref, sin_ref,
                   fq_ref, fk_ref, fv_ref, fog_ref, lf_ref, rq_ref, rk_ref, rv_ref, rg_ref):
    lo = _half_lane_mask()
    w = FOX_WIDTH
    rw = RET_WIDTH
    tm = x_ref.shape[1]
    sub = min(TOKEN_TILE, tm)

    def qk_norm(p, gain_ref):
        outs = []
        for g in range(w // LANES):
            v = p[:, g * LANES:(g + 1) * LANES]
            ss = _pair_sumsq(v, lo)
            outs.append(v * lax.rsqrt(ss * (1.0 / FOX_HEAD_DIM) + EPS))
        return (jnp.concatenate(outs, axis=-1) * gain_ref[...]).astype(bf16)

    def rope(p, cos, sin, scale):
        outs = []
        for g in range(RET_HEADS):
            v = p[:, g * LANES:(g + 1) * LANES]
            outs.append(v * cos + pltpu.roll(v, RET_HEAD_DIM // 2, 1) * sin)
        r = jnp.concatenate(outs, axis=-1)
        return (r * scale if scale != 1.0 else r).astype(bf16)

    for r in range(tm // sub):
        rs = slice(r * sub, (r + 1) * sub)
        x = x_ref[0, rs, :]
        inv = lax.rsqrt(jnp.mean(x * x, axis=-1, keepdims=True) + EPS)
        h = (x * inv * (1.0 + mod_ref[0, 1:2, :]) + mod_ref[0, 0:1, :]).astype(bf16)

        def proj(w_ref, start, n):
            return _dot(h, w_ref[:, start:start + n])

        fq_ref[0, rs, :] = qk_norm(proj(wf_ref, 0, w), gq_ref)
        fk_ref[0, rs, :] = qk_norm(proj(wf_ref, w, w), gk_ref)
        fog_ref[0, rs, :] = jax.nn.sigmoid(proj(wf_ref, 3 * w, w)).astype(bf16)

        z = proj(wg_ref, 0, LANES) + bf_ref[...]
        log_f = jnp.minimum(z, 0.0) - jnp.log(1.0 + jnp.exp(-jnp.abs(z)))
        lf_ref[0, :, rs] = log_f.T[0:FOX_HEADS, :]

        cos = cos_ref[rs, :]
        sin = sin_ref[rs, :]
        rq_ref[0, rs, :] = rope(proj(wr_ref, 0, rw), cos, sin, 1.0)
        rk_ref[0, rs, :] = rope(proj(wr_ref, rw, rw), cos, sin, RET_HEAD_DIM ** -0.5)
        gate = proj(wr_ref, 3 * rw, rw)
        rg_ref[0, rs, :] = (gate * jax.nn.sigmoid(gate)).astype(bf16)
        rv_ref[0, rs, :] = proj(wr_ref, 2 * rw, rw).astype(bf16)
        fv_ref[0, rs, :] = proj(wf_ref, 2 * w, w).astype(bf16)


def _input_projection(x, mod, w_in, b_forget, q_gain, k_gain):
    b, s, d = x.shape
    tm = min(2 * TOKEN_TILE, s)
    o_ff = 4 * FOX_WIDTH
    o_r = o_ff + FOX_HEADS
    w_fox = w_in[:, :o_ff].astype(bf16)
    w_fg = jnp.pad(w_in[:, o_ff:o_r], ((0, 0), (0, LANES - FOX_HEADS))).astype(bf16)
    w_ret = w_in[:, o_r:].astype(bf16)
    bias_f = jnp.pad(b_forget, (0, LANES - FOX_HEADS)).reshape(1, LANES)
    gq = (jnp.tile(q_gain, FOX_HEADS) * (LOG2E * FOX_HEAD_DIM ** -0.5)).reshape(1, FOX_WIDTH)
    gk = jnp.tile(k_gain, FOX_HEADS).reshape(1, FOX_WIDTH)

    pos = np.arange(s, dtype=np.float32)
    inv_freq = np.float32(ROPE_BASE) ** (-np.arange(0, RET_HEAD_DIM, 2, dtype=np.float32) / np.float32(RET_HEAD_DIM))
    ang = pos[:, None] * inv_freq[None, :]
    cos_t = np.concatenate([np.cos(ang), np.cos(ang)], axis=-1).astype(np.float32)
    sin_t = np.concatenate([-np.sin(ang), np.sin(ang)], axis=-1).astype(np.float32)

    tok = lambda width: pl.BlockSpec((1, tm, width), lambda i, j: (i, j, 0))
    full = lambda a: pl.BlockSpec(a.shape, lambda i, j: (0,) * a.ndim, pipeline_mode=pl.Buffered(1))
    act = lambda width: jax.ShapeDtypeStruct((b, s, width), bf16)
    return pl.pallas_call(
        _inproj_kernel,
        grid=(b, s // tm),
        in_specs=[
            tok(d),
            pl.BlockSpec((1, N_MOD, d), lambda i, j: (i, 0, 0)),
            full(w_fox), full(w_fg), full(w_ret), full(gq), full(gk), full(bias_f),
            pl.BlockSpec((tm, LANES), lambda i, j: (j, 0)),
            pl.BlockSpec((tm, LANES), lambda i, j: (j, 0)),
        ],
        out_specs=[
            tok(FOX_WIDTH), tok(FOX_WIDTH), tok(FOX_WIDTH), tok(FOX_WIDTH),
            pl.BlockSpec((1, FOX_HEADS, tm), lambda i, j: (i, 0, j)),
            tok(RET_WIDTH), tok(RET_WIDTH), tok(RET_WIDTH), tok(RET_WIDTH),
        ],
        out_shape=[
            act(FOX_WIDTH), act(FOX_WIDTH), act(FOX_WIDTH), act(FOX_WIDTH),
            jax.ShapeDtypeStruct((b, FOX_HEADS, s), f32),
            act(RET_WIDTH), act(RET_WIDTH), act(RET_WIDTH), act(RET_WIDTH),
        ],
        compiler_params=pltpu.CompilerParams(
            dimension_semantics=("arbitrary", "arbitrary"), vmem_limit_bytes=VMEM_LIMIT),
        name="in_proj",
    )(x, mod, w_fox, w_fg, w_ret, gq, gk, bias_f, cos_t, sin_t)


def _lane_cumsum(x, out_ref):
    rows, s = x.shape
    r = lax.broadcasted_iota(jnp.int32, (LANES, LANES), 0)
    c = lax.broadcasted_iota(jnp.int32, (LANES, LANES), 1)
    tri = (r <= c).astype(bf16)
    hi = x.astype(bf16).astype(f32)
    rest = x - hi
    mid = rest.astype(bf16).astype(f32)
    low = (rest - mid).astype(bf16).astype(f32)
    n = s // LANES
    chunks = [slice(ch * LANES, (ch + 1) * LANES) for ch in range(n)]
    pieces = jnp.concatenate([g[:, sl] for g in (hi, mid, low) for sl in chunks], axis=0)
    rhs = jnp.concatenate([tri, jnp.ones((LANES, LANES), bf16)], axis=1)
    part = _dot(pieces.astype(bf16), rhs)
    offset = jnp.zeros((rows, LANES), f32)
    for ch, sl in enumerate(chunks):
        hi_c, mid_c, low_c = (part[(g * n + ch) * rows:(g * n + ch + 1) * rows] for g in range(3))
        both = hi_c + mid_c + low_c
        out_ref[:, sl] = both[:, 0:LANES] + offset
        offset = offset + both[:, LANES:]


def _fox_kernel(q_ref, k_ref, v_ref, lf_ref, og_ref, gain_ref, o_ref, cum_ref, dec_ref, s_ref):
    pair = pl.program_id(1)
    s = q_ref.shape[1]
    t = min(ATTN_TILE, s)

    @pl.when(pair == 0)
    def _():
        _lane_cumsum(lf_ref[0] * LOG2E, cum_ref)
        neg = -cum_ref[...]
        hi = neg.astype(bf16).astype(f32)
        mid = (neg - hi).astype(bf16).astype(f32)
        low = (neg - hi - mid).astype(bf16).astype(f32)
        sub = lax.broadcasted_iota(jnp.int32, (3 * FOX_HEADS, 1), 0)
        decay_t = jnp.zeros((3 * FOX_HEADS, s), f32)
        for h in range(FOX_HEADS):
            for g, piece in enumerate((hi, mid, low)):
                decay_t = jnp.where(sub == 3 * h + g, piece[h:h + 1], decay_t)
        decay_t = jnp.concatenate([decay_t, jnp.zeros((LANES - 3 * FOX_HEADS, s), f32)], axis=0)
        dec_ref[...] = decay_t.T.astype(bf16)

    lo = _half_lane_mask()
    row = lax.broadcasted_iota(jnp.int32, (t, t), 0)
    col = lax.broadcasted_iota(jnp.int32, (t, t), 1)
    causal = row >= col
    lane = lax.broadcasted_iota(jnp.int32, (1, LANES), 1)
    first = 6 * pair
    ones_a = jnp.where((lane >= first) & (lane < first + 3), 1.0, 0.0).astype(bf16)
    ones_b = jnp.where((lane >= first + 3) & (lane < first + 6), 1.0, 0.0).astype(bf16)
    nq = s // t
    row_max, pv_acc = {}, {}

    def score_pass(qi):
        q = q_ref[0, qi * t:(qi + 1) * t, :]
        zero = jnp.zeros_like(q)
        q2 = jnp.concatenate(
            [jnp.concatenate([jnp.where(lo, q, zero), jnp.broadcast_to(ones_a, q.shape)], axis=1),
             jnp.concatenate([jnp.where(lo, zero, q), jnp.broadcast_to(ones_b, q.shape)], axis=1)], axis=0)
        m_run = jnp.full((2 * t, LANES), MASK_VALUE, f32)
        for kb in range(qi + 1):
            ks = slice(kb * t, (kb + 1) * t)
            keys = jnp.concatenate([k_ref[0, ks, :], dec_ref[ks, :]], axis=1)
            sc = _dot_nt(q2, keys)
            if kb == qi:
                sc = jnp.concatenate([jnp.where(causal, sc[0:t], MASK_VALUE),
                                      jnp.where(causal, sc[t:], MASK_VALUE)], axis=0)
            s_ref[qi % 2, :, ks] = sc
            for j in range(t // LANES):
                m_run = jnp.maximum(m_run, sc[:, j * LANES:(j + 1) * LANES])
            yield
        row_max[qi] = jnp.max(m_run, axis=-1, keepdims=True)

    def prob_pass(qi):
        m = row_max.pop(qi)
        acc = None
        for kb in range(qi + 1):
            ks = slice(kb * t, (kb + 1) * t)
            p = jnp.exp2(s_ref[qi % 2, :, ks] - m).astype(bf16)
            v1 = jnp.concatenate([v_ref[0, ks, :], jnp.ones((t, LANES), bf16)], axis=1)
            part = _dot(p, v1)
            acc = part if acc is None else acc + part
            yield
        pv_acc[qi] = acc

    def finish(qi):
        rows = slice(qi * t, (qi + 1) * t)
        pv = pv_acc.pop(qi)
        o2 = pv[:, 0:LANES] / pv[:, LANES:]
        out = jnp.where(lo, o2[0:t], o2[t:])
        ss = _pair_sumsq(out, lo)
        out = out * lax.rsqrt(ss * (1.0 / FOX_HEAD_DIM) + EPS) * gain_ref[0]
        o_ref[0, rows, :] = (out * og_ref[0, rows, :].astype(f32)).astype(bf16)

    def interleave(*passes):
        live = list(passes)
        while live:
            live = [g for g in live if next(g, StopIteration) is not StopIteration]

    interleave(score_pass(nq - 1))
    for qi in reversed(range(nq)):
        if qi + 1 < nq:
            finish(qi + 1)
        interleave(*([score_pass(qi - 1)] if qi >= 1 else []), prob_pass(qi))
    finish(0)


def _fox_attention(fq, fk, fv, fog, log_f, out_gain):
    b, s, _ = fq.shape
    t = min(ATTN_TILE, s)
    pairs = FOX_WIDTH // LANES
    gain = out_gain.reshape(pairs, 1, LANES)
    seq = pl.BlockSpec((1, s, LANES), lambda i, p: (i, 0, p))
    return pl.pallas_call(
        _fox_kernel,
        grid=(b, pairs),
        in_specs=[
            seq, seq, seq,
            pl.BlockSpec((1, FOX_HEADS, s), lambda i, p: (i, 0, 0)),
            seq,
            pl.BlockSpec((1, 1, LANES), lambda i, p: (p, 0, 0)),
        ],
        out_specs=seq,
        out_shape=jax.ShapeDtypeStruct((b, s, FOX_WIDTH), bf16),
        scratch_shapes=[
            pltpu.VMEM((FOX_HEADS, s), f32),
            pltpu.VMEM((s, LANES), bf16),
            pltpu.VMEM((2, 2 * t, s), f32),
        ],
        compiler_params=pltpu.CompilerParams(
            dimension_semantics=("arbitrary", "arbitrary"), vmem_limit_bytes=VMEM_LIMIT),
        name="fox_attention",
    )(fq, fk, fv, log_f, fog, gain)


def _ret_kernel(q_ref, k_ref, v_ref, g_ref, cst_ref, gain_ref, o_ref, inner_ref, kv_ref, st_ref):
    s = q_ref.shape[1]
    c = RET_CHUNK
    heads = [slice(hd * RET_HEAD_DIM, (hd + 1) * RET_HEAD_DIM) for hd in range(RET_HEADS)]
    chunks = [slice(ci * c, (ci + 1) * c) for ci in range(s // c)]

    for hd, cs in enumerate(heads):
        for ci, rs in enumerate(chunks):
            k = k_ref[0, rs, cs]
            inner_ref[rs, cs] = (_dot_nt(q_ref[0, rs, cs], k) * cst_ref[hd, 0]).astype(bf16)
            kz = (k.astype(f32) * cst_ref[hd, 2]).T.astype(bf16)
            kv_ref[ci, :, cs] = _dot(kz, v_ref[0, rs, cs])

    for hd, cs in enumerate(heads):
        state = jnp.zeros((RET_HEAD_DIM, RET_HEAD_DIM), f32)
        for ci in range(len(chunks)):
            st_ref[ci, :, cs] = state.astype(bf16)
            state = state * cst_ref[hd, 3] + kv_ref[ci, :, cs]

    for hd, cs in enumerate(heads):
        for ci, rs in enumerate(chunks):
            q_xi = (q_ref[0, rs, cs].astype(f32) * cst_ref[hd, 1]).astype(bf16)
            lhs = jnp.concatenate([inner_ref[rs, cs], q_xi], axis=1)
            rhs = jnp.concatenate([v_ref[0, rs, cs], st_ref[ci, :, cs]], axis=0)
            out = _dot(lhs, rhs)
            inv = lax.rsqrt(jnp.mean(out * out, axis=-1, keepdims=True) + EPS)
            gate = g_ref[0, rs, cs].astype(f32)
            o_ref[0, rs, cs] = (out * inv * gain_ref[:, cs] * gate).astype(bf16)


def _retention_constants():
    c = RET_CHUNK
    f = np.float32
    log_g = np.log(f(1.0) - f(2.0) ** (f(-5.0) - np.arange(RET_HEADS, dtype=f)))
    n = np.arange(c, dtype=f)
    diff = n[:, None] - n[None, :]
    mask = np.where(diff[None] >= 0, np.exp(np.maximum(diff, f(0.0))[None] * log_g[:, None, None]), f(0.0))
    xi = np.exp((n[None, :] + f(1.0)) * log_g[:, None])
    zeta = np.exp((f(c) - f(1.0) - n[None, :]) * log_g[:, None])
    g_chunk = np.exp(f(c) * log_g)
    bc = lambda rows: np.broadcast_to(rows[:, :, None], (RET_HEADS, c, c))
    return np.stack([mask, bc(xi), bc(zeta), np.broadcast_to(g_chunk[:, None, None], (RET_HEADS, c, c))], axis=1).astype(f)


def _retention(rq, rk, rv, rg, out_gain):
    b, s, _ = rq.shape
    cst = _retention_constants()
    gain = out_gain.reshape(1, RET_WIDTH)
    seq = pl.BlockSpec((1, s, RET_WIDTH), lambda i: (i, 0, 0))
    return pl.pallas_call(
        _ret_kernel,
        grid=(b,),
        in_specs=[seq, seq, seq, seq,
                  pl.BlockSpec(cst.shape, lambda i: (0, 0, 0, 0)),
                  pl.BlockSpec(gain.shape, lambda i: (0, 0))],
        out_specs=seq,
        out_shape=jax.ShapeDtypeStruct((b, s, RET_WIDTH), bf16),
        scratch_shapes=[
            pltpu.VMEM((s, RET_WIDTH), bf16),
            pltpu.VMEM((s // RET_CHUNK, RET_HEAD_DIM, RET_WIDTH), f32),
            pltpu.VMEM((s // RET_CHUNK, RET_HEAD_DIM, RET_WIDTH), bf16),
        ],
        compiler_params=pltpu.CompilerParams(
            dimension_semantics=("arbitrary",), vmem_limit_bytes=VMEM_LIMIT),
        name="retention",
    )(rq, rk, rv, rg, cst, gain)


def _out_mlp_kernel(x_ref, mf_ref, mr_ref, mod_ref, wo_ref, w1_ref, w2_ref, o_ref):
    gate_m = mod_ref[0, 2:3, :]
    shift_f = mod_ref[0, 3:4, :]
    scale_f = mod_ref[0, 4:5, :]
    gate_f = mod_ref[0, 5:6, :]
    d_ff = w1_ref.shape[1]
    fc = min(FF_CHUNK, d_ff)
    tm = x_ref.shape[1]
    sub = min(TOKEN_TILE, tm)
    for r in range(tm // sub):
        rs = slice(r * sub, (r + 1) * sub)
        mixed = _dot(mf_ref[0, rs, :], wo_ref[0:FOX_WIDTH, :]) + _dot(mr_ref[0, rs, :], wo_ref[FOX_WIDTH:, :])
        x1 = x_ref[0, rs, :] + gate_m * mixed
        inv = lax.rsqrt(jnp.mean(x1 * x1, axis=-1, keepdims=True) + EPS)
        h = (x1 * inv * (1.0 + scale_f) + shift_f).astype(bf16)
        y = jnp.zeros(x1.shape, f32)
        for j in range(d_ff // fc):
            u = jnp.maximum(_dot(h, w1_ref[:, j * fc:(j + 1) * fc]), 0.0)
            y = y + _dot((u * u).astype(bf16), w2_ref[j * fc:(j + 1) * fc, :])
        o_ref[0, rs, :] = x1 + gate_f * y


def _out_mlp(x, mixed_fox, mixed_ret, mod, w_out, w_mlp_in, w_mlp_out):
    b, s, d = x.shape
    tm = min(2 * TOKEN_TILE, s)
    wo = w_out.astype(bf16)
    w1 = w_mlp_in.astype(bf16)
    w2 = w_mlp_out.astype(bf16)
    tok = lambda width: pl.BlockSpec((1, tm, width), lambda i, j: (i, j, 0))
    resident = lambda a: pl.BlockSpec(a.shape, lambda i, j: (0, 0), pipeline_mode=pl.Buffered(1))
    return pl.pallas_call(
        _out_mlp_kernel,
        grid=(b, s // tm),
        in_specs=[
            tok(d), tok(FOX_WIDTH), tok(RET_WIDTH),
            pl.BlockSpec((1, N_MOD, d), lambda i, j: (i, 0, 0)),
            resident(wo), resident(w1), resident(w2),
        ],
        out_specs=tok(d),
        out_shape=jax.ShapeDtypeStruct((b, s, d), f32),
        compiler_params=pltpu.CompilerParams(
            dimension_semantics=("arbitrary", "arbitrary"), vmem_limit_bytes=VMEM_LIMIT),
        name="out_mlp",
    )(x, mixed_fox, mixed_ret, mod, wo, w1, w2)


def kernel(x, c, w_ada, b_ada, w_in, b_forget, q_norm_gain, k_norm_gain, fox_out_gain, ret_out_gain,
           w_out, w_mlp_in, w_mlp_out):
    b, s, d = x.shape
    for l in range(w_ada.shape[0]):
        mod = _modulation(c, w_ada[l], b_ada[l]).reshape(b, N_MOD, d)
        fq, fk, fv, fog, log_f, rq, rk, rv, rg = _input_projection(
            x, mod, w_in[l], b_forget[l], q_norm_gain[l], k_norm_gain[l])
        mixed_fox = _fox_attention(fq, fk, fv, fog, log_f, fox_out_gain[l])
        mixed_ret = _retention(rq, rk, rv, rg, ret_out_gain[l])
        x = _out_mlp(x, mixed_fox, mixed_ret, mod, w_out[l], w_mlp_in[l], w_mlp_out[l])
    return x
```

```python
import numpy as np

import jax
import jax.numpy as jnp
from jax import lax
from jax.experimental import pallas as pl
from jax.experimental.pallas import tpu as pltpu

FOX_HEADS = 8
FOX_HEAD_DIM = 64
FOX_WIDTH = FOX_HEADS * FOX_HEAD_DIM
RET_HEADS = 4
RET_HEAD_DIM = 128
RET_WIDTH = RET_HEADS * RET_HEAD_DIM
RET_CHUNK = 128
ROPE_BASE = 10000.0
EPS = 1e-6
N_MOD = 6

LANES = 128
VMEM_LIMIT = 56 * 1024 * 1024
MASK_VALUE = -1e30
LOG2E = 1.4426950408889634

TOKEN_TILE = 512
ATTN_TILE = 256
FF_CHUNK = 1024

bf16 = jnp.bfloat16
f32 = jnp.float32


def _dot(a, b):
    return jnp.dot(a, b, preferred_element_type=f32)


def _dot_nt(a, b):
    return lax.dot_general(a, b, (((1,), (1,)), ((), ())), preferred_element_type=f32)


def _half_lane_mask():
    return lax.broadcasted_iota(jnp.int32, (1, LANES), 1) < FOX_HEAD_DIM


def _pair_sumsq(v, lo):
    sq = v * v
    s_lo = jnp.sum(jnp.where(lo, sq, 0.0), axis=-1, keepdims=True)
    s_hi = jnp.sum(jnp.where(lo, 0.0, sq), axis=-1, keepdims=True)
    return jnp.where(lo, s_lo, s_hi)


def _mod_kernel(c_ref, w_ref, b_ref, o_ref):
    c = c_ref[...]
    c_act = (c * jax.nn.sigmoid(c)).astype(bf16)
    o_ref[...] = _dot(c_act, w_ref[...].astype(bf16)) + b_ref[...]


def _modulation(c, w_ada, b_ada):
    b, d = c.shape
    n = w_ada.shape[1]
    return pl.pallas_call(
        _mod_kernel,
        grid=(n // d,),
        in_specs=[
            pl.BlockSpec((b, d), lambda j: (0, 0)),
            pl.BlockSpec((d, d), lambda j: (0, j)),
            pl.BlockSpec((1, d), lambda j: (0, j)),
        ],
        out_specs=pl.BlockSpec((b, d), lambda j: (0, j)),
        out_shape=jax.ShapeDtypeStruct((b, n), f32),
        compiler_params=pltpu.CompilerParams(
            dimension_semantics=("arbitrary",), vmem_limit_bytes=VMEM_LIMIT),
        name="adaln_mod",
    )(c, w_ada, b_ada.reshape(1, n))


def _inproj_kernel(x_ref, mod_ref, wf_ref, wg_ref, wr_ref, gq_ref, gk_ref, bf_ref, cos_ref, sin_ref,
                   fox_ref, lf_ref, ret_ref):
    lo = _half_lane_mask()
    w = FOX_WIDTH
    rw = RET_WIDTH
    tm = x_ref.shape[1]
    sub = min(TOKEN_TILE, tm)

    def qk_norm(p, gain_ref):
        outs = []
        for g in range(w // LANES):
            v = p[:, g * LANES:(g + 1) * LANES]
            ss = _pair_sumsq(v, lo)
            outs.append(v * lax.rsqrt(ss * (1.0 / FOX_HEAD_DIM) + EPS))
        return (jnp.concatenate(outs, axis=-1) * gain_ref[...]).astype(bf16)

    def rope(p, cos, sin, scale):
        outs = []
        for g in range(RET_HEADS):
            v = p[:, g * LANES:(g + 1) * LANES]
            outs.append(v * cos + pltpu.roll(v, RET_HEAD_DIM // 2, 1) * sin)
        r = jnp.concatenate(outs, axis=-1)
        return (r * scale if scale != 1.0 else r).astype(bf16)

    for r in range(tm // sub):
        rs = slice(r * sub, (r + 1) * sub)
        x = x_ref[0, rs, :]
        inv = lax.rsqrt(jnp.mean(x * x, axis=-1, keepdims=True) + EPS)
        h = (x * inv * (1.0 + mod_ref[0, 1:2, :]) + mod_ref[0, 0:1, :]).astype(bf16)

        def proj(w_ref, start, n):
            return _dot(h, w_ref[:, start:start + n])

        fox_ref[0, rs, 0:w] = qk_norm(proj(wf_ref, 0, w), gq_ref)
        fox_ref[0, rs, w:2 * w] = qk_norm(proj(wf_ref, w, w), gk_ref)
        fox_ref[0, rs, 3 * w:4 * w] = jax.nn.sigmoid(proj(wf_ref, 3 * w, w)).astype(bf16)

        z = proj(wg_ref, 0, LANES) + bf_ref[...]
        log_f = jnp.minimum(z, 0.0) - jnp.log(1.0 + jnp.exp(-jnp.abs(z)))
        lf_ref[0, :, rs] = log_f.T[0:FOX_HEADS, :]

        cos = cos_ref[rs, :]
        sin = sin_ref[rs, :]
        ret_ref[0, rs, 0:rw] = rope(proj(wr_ref, 0, rw), cos, sin, 1.0)
        ret_ref[0, rs, rw:2 * rw] = rope(proj(wr_ref, rw, rw), cos, sin, RET_HEAD_DIM ** -0.5)
        gate = proj(wr_ref, 3 * rw, rw)
        ret_ref[0, rs, 3 * rw:4 * rw] = (gate * jax.nn.sigmoid(gate)).astype(bf16)
        ret_ref[0, rs, 2 * rw:3 * rw] = proj(wr_ref, 2 * rw, rw).astype(bf16)
        fox_ref[0, rs, 2 * w:3 * w] = proj(wf_ref, 2 * w, w).astype(bf16)


def _input_projection(x, mod, w_in, b_forget, q_gain, k_gain):
    b, s, d = x.shape
    tm = min(2 * TOKEN_TILE, s)
    o_ff = 4 * FOX_WIDTH
    o_r = o_ff + FOX_HEADS
    w_fox = w_in[:, :o_ff].astype(bf16)
    w_fg = jnp.pad(w_in[:, o_ff:o_r], ((0, 0), (0, LANES - FOX_HEADS))).astype(bf16)
    w_ret = w_in[:, o_r:].astype(bf16)
    bias_f = jnp.pad(b_forget, (0, LANES - FOX_HEADS)).reshape(1, LANES)
    gq = (jnp.tile(q_gain, FOX_HEADS) * (LOG2E * FOX_HEAD_DIM ** -0.5)).reshape(1, FOX_WIDTH)
    gk = jnp.tile(k_gain, FOX_HEADS).reshape(1, FOX_WIDTH)

    pos = np.arange(s, dtype=np.float32)
    inv_freq = np.float32(ROPE_BASE) ** (-np.arange(0, RET_HEAD_DIM, 2, dtype=np.float32) / np.float32(RET_HEAD_DIM))
    ang = pos[:, None] * inv_freq[None, :]
    cos_t = np.concatenate([np.cos(ang), np.cos(ang)], axis=-1).astype(np.float32)
    sin_t = np.concatenate([-np.sin(ang), np.sin(ang)], axis=-1).astype(np.float32)

    tok = lambda width: pl.BlockSpec((1, tm, width), lambda i, j: (i, j, 0))
    full = lambda a: pl.BlockSpec(a.shape, lambda i, j: (0,) * a.ndim, pipeline_mode=pl.Buffered(1))
    act = lambda width: jax.ShapeDtypeStruct((b, s, width), bf16)
    return pl.pallas_call(
        _inproj_kernel,
        grid=(b, s // tm),
        in_specs=[
            tok(d),
            pl.BlockSpec((1, N_MOD, d), lambda i, j: (i, 0, 0)),
            full(w_fox), full(w_fg), full(w_ret), full(gq), full(gk), full(bias_f),
            pl.BlockSpec((tm, LANES), lambda i, j: (j, 0)),
            pl.BlockSpec((tm, LANES), lambda i, j: (j, 0)),
        ],
        out_specs=[
            tok(4 * FOX_WIDTH),
            pl.BlockSpec((1, FOX_HEADS, tm), lambda i, j: (i, 0, j)),
            tok(4 * RET_WIDTH),
        ],
        out_shape=[
            act(4 * FOX_WIDTH),
            jax.ShapeDtypeStruct((b, FOX_HEADS, s), f32),
            act(4 * RET_WIDTH),
        ],
        compiler_params=pltpu.CompilerParams(
            dimension_semantics=("arbitrary", "arbitrary"), vmem_limit_bytes=VMEM_LIMIT),
        name="in_proj",
    )(x, mod, w_fox, w_fg, w_ret, gq, gk, bias_f, cos_t, sin_t)


def _lane_cumsum(x, out_ref):
    rows, s = x.shape
    r = lax.broadcasted_iota(jnp.int32, (LANES, LANES), 0)
    c = lax.broadcasted_iota(jnp.int32, (LANES, LANES), 1)
    tri = (r <= c).astype(bf16)
    hi = x.astype(bf16).astype(f32)
    rest = x - hi
    mid = rest.astype(bf16).astype(f32)
    low = (rest - mid).astype(bf16).astype(f32)
    n = s // LANES
    chunks = [slice(ch * LANES, (ch + 1) * LANES) for ch in range(n)]
    pieces = jnp.concatenate([g[:, sl] for g in (hi, mid, low) for sl in chunks], axis=0)
    rhs = jnp.concatenate([tri, jnp.ones((LANES, LANES), bf16)], axis=1)
    part = _dot(pieces.astype(bf16), rhs)
    offset = jnp.zeros((rows, LANES), f32)
    for ch, sl in enumerate(chunks):
        hi_c, mid_c, low_c = (part[(g * n + ch) * rows:(g * n + ch + 1) * rows] for g in range(3))
        both = hi_c + mid_c + low_c
        out_ref[:, sl] = both[:, 0:LANES] + offset
        offset = offset + both[:, LANES:]


def _fox_kernel(q_ref, k_ref, v_ref, lf_ref, og_ref, gain_ref, o_ref, cum_ref, dec_ref, s_ref):
    pair = pl.program_id(1)
    s = q_ref.shape[1]
    t = min(ATTN_TILE, s)

    @pl.when(pair == 0)
    def _():
        _lane_cumsum(lf_ref[0] * LOG2E, cum_ref)
        neg = -cum_ref[...]
        hi = neg.astype(bf16).astype(f32)
        mid = (neg - hi).astype(bf16).astype(f32)
        low = (neg - hi - mid).astype(bf16).astype(f32)
        sub = lax.broadcasted_iota(jnp.int32, (3 * FOX_HEADS, 1), 0)
        decay_t = jnp.zeros((3 * FOX_HEADS, s), f32)
        for h in range(FOX_HEADS):
            for g, piece in enumerate((hi, mid, low)):
                decay_t = jnp.where(sub == 3 * h + g, piece[h:h + 1], decay_t)
        decay_t = jnp.concatenate([decay_t, jnp.zeros((LANES - 3 * FOX_HEADS, s), f32)], axis=0)
        dec_ref[...] = decay_t.T.astype(bf16)

    lo = _half_lane_mask()
    row = lax.broadcasted_iota(jnp.int32, (t, t), 0)
    col = lax.broadcasted_iota(jnp.int32, (t, t), 1)
    causal = row >= col
    lane = lax.broadcasted_iota(jnp.int32, (1, LANES), 1)
    first = 6 * pair
    ones_a = jnp.where((lane >= first) & (lane < first + 3), 1.0, 0.0).astype(bf16)
    ones_b = jnp.where((lane >= first + 3) & (lane < first + 6), 1.0, 0.0).astype(bf16)
    nq = s // t
    row_max, pv_acc = {}, {}

    def score_pass(qi):
        q = q_ref[0, qi * t:(qi + 1) * t, :]
        zero = jnp.zeros_like(q)
        q2 = jnp.concatenate(
            [jnp.concatenate([jnp.where(lo, q, zero), jnp.broadcast_to(ones_a, q.shape)], axis=1),
             jnp.concatenate([jnp.where(lo, zero, q), jnp.broadcast_to(ones_b, q.shape)], axis=1)], axis=0)
        m_run = jnp.full((2 * t, LANES), MASK_VALUE, f32)
        for kb in range(qi + 1):
            ks = slice(kb * t, (kb + 1) * t)
            keys = jnp.concatenate([k_ref[0, ks, :], dec_ref[ks, :]], axis=1)
            sc = _dot_nt(q2, keys)
            if kb == qi:
                sc = jnp.concatenate([jnp.where(causal, sc[0:t], MASK_VALUE),
                                      jnp.where(causal, sc[t:], MASK_VALUE)], axis=0)
            s_ref[qi % 2, :, ks] = sc
            for j in range(t // LANES):
                m_run = jnp.maximum(m_run, sc[:, j * LANES:(j + 1) * LANES])
            yield
        row_max[qi] = jnp.max(m_run, axis=-1, keepdims=True)

    def prob_pass(qi):
        m = row_max.pop(qi)
        acc = None
        for kb in range(qi + 1):
            ks = slice(kb * t, (kb + 1) * t)
            p = jnp.exp2(s_ref[qi % 2, :, ks] - m).astype(bf16)
            v1 = jnp.concatenate([v_ref[0, ks, :], jnp.ones((t, LANES), bf16)], axis=1)
            part = _dot(p, v1)
            acc = part if acc is None else acc + part
            yield
        pv_acc[qi] = acc

    def finish(qi):
        rows = slice(qi * t, (qi + 1) * t)
        pv = pv_acc.pop(qi)
        o2 = pv[:, 0:LANES] / pv[:, LANES:]
        out = jnp.where(lo, o2[0:t], o2[t:])
        ss = _pair_sumsq(out, lo)
        out = out * lax.rsqrt(ss * (1.0 / FOX_HEAD_DIM) + EPS) * gain_ref[0]
        o_ref[0, rows, :] = (out * og_ref[0, rows, :].astype(f32)).astype(bf16)

    def interleave(*passes):
        live = list(passes)
        while live:
            live = [g for g in live if next(g, StopIteration) is not StopIteration]

    interleave(score_pass(nq - 1))
    for qi in reversed(range(nq)):
        if qi + 1 < nq:
            finish(qi + 1)
        interleave(*([score_pass(qi - 1)] if qi >= 1 else []), prob_pass(qi))
    finish(0)


def _fox_attention(fox_act, log_f, out_gain):
    b, s, _ = fox_act.shape
    t = min(ATTN_TILE, s)
    pairs = FOX_WIDTH // LANES
    gain = out_gain.reshape(pairs, 1, LANES)
    seq = pl.BlockSpec((1, s, LANES), lambda i, p: (i, 0, p))
    part = lambda n: pl.BlockSpec((1, s, LANES), lambda i, p: (i, 0, n * pairs + p))
    return pl.pallas_call(
        _fox_kernel,
        grid=(b, pairs),
        in_specs=[
            part(0), part(1), part(2),
            pl.BlockSpec((1, FOX_HEADS, s), lambda i, p: (i, 0, 0)),
            part(3),
            pl.BlockSpec((1, 1, LANES), lambda i, p: (p, 0, 0)),
        ],
        out_specs=seq,
        out_shape=jax.ShapeDtypeStruct((b, s, FOX_WIDTH), bf16),
        scratch_shapes=[
            pltpu.VMEM((FOX_HEADS, s), f32),
            pltpu.VMEM((s, LANES), bf16),
            pltpu.VMEM((2, 2 * t, s), f32),
        ],
        compiler_params=pltpu.CompilerParams(
            dimension_semantics=("arbitrary", "arbitrary"), vmem_limit_bytes=VMEM_LIMIT),
        name="fox_attention",
    )(fox_act, fox_act, fox_act, log_f, fox_act, gain)


def _ret_kernel(q_ref, k_ref, v_ref, g_ref, cst_ref, gain_ref, o_ref, inner_ref, kv_ref, st_ref):
    s = q_ref.shape[1]
    c = RET_CHUNK
    heads = [slice(hd * RET_HEAD_DIM, (hd + 1) * RET_HEAD_DIM) for hd in range(RET_HEADS)]
    chunks = [slice(ci * c, (ci + 1) * c) for ci in range(s // c)]

    for hd, cs in enumerate(heads):
        for ci, rs in enumerate(chunks):
            k = k_ref[0, rs, cs]
            inner_ref[rs, cs] = (_dot_nt(q_ref[0, rs, cs], k) * cst_ref[hd, 0]).astype(bf16)
            kz = (k.astype(f32) * cst_ref[hd, 2]).T.astype(bf16)
            kv_ref[ci, :, cs] = _dot(kz, v_ref[0, rs, cs])

    for hd, cs in enumerate(heads):
        state = jnp.zeros((RET_HEAD_DIM, RET_HEAD_DIM), f32)
        for ci in range(len(chunks)):
            st_ref[ci, :, cs] = state.astype(bf16)
            state = state * cst_ref[hd, 3] + kv_ref[ci, :, cs]

    for hd, cs in enumerate(heads):
        for ci, rs in enumerate(chunks):
            q_xi = (q_ref[0, rs, cs].astype(f32) * cst_ref[hd, 1]).astype(bf16)
            lhs = jnp.concatenate([inner_ref[rs, cs], q_xi], axis=1)
            rhs = jnp.concatenate([v_ref[0, rs, cs], st_ref[ci, :, cs]], axis=0)
            out = _dot(lhs, rhs)
            inv = lax.rsqrt(jnp.mean(out * out, axis=-1, keepdims=True) + EPS)
            gate = g_ref[0, rs, cs].astype(f32)
            o_ref[0, rs, cs] = (out * inv * gain_ref[:, cs] * gate).astype(bf16)


def _retention_constants():
    c = RET_CHUNK
    f = np.float32
    log_g = np.log(f(1.0) - f(2.0) ** (f(-5.0) - np.arange(RET_HEADS, dtype=f)))
    n = np.arange(c, dtype=f)
    diff = n[:, None] - n[None, :]
    mask = np.where(diff[None] >= 0, np.exp(np.maximum(diff, f(0.0))[None] * log_g[:, None, None]), f(0.0))
    xi = np.exp((n[None, :] + f(1.0)) * log_g[:, None])
    zeta = np.exp((f(c) - f(1.0) - n[None, :]) * log_g[:, None])
    g_chunk = np.exp(f(c) * log_g)
    bc = lambda rows: np.broadcast_to(rows[:, :, None], (RET_HEADS, c, c))
    return np.stack([mask, bc(xi), bc(zeta), np.broadcast_to(g_chunk[:, None, None], (RET_HEADS, c, c))], axis=1).astype(f)


def _retention(ret_act, out_gain):
    b, s, _ = ret_act.shape
    cst = _retention_constants()
    gain = out_gain.reshape(1, RET_WIDTH)
    seq = pl.BlockSpec((1, s, RET_WIDTH), lambda i: (i, 0, 0))
    part = lambda n: pl.BlockSpec((1, s, RET_WIDTH), lambda i: (i, 0, n))
    return pl.pallas_call(
        _ret_kernel,
        grid=(b,),
        in_specs=[part(0), part(1), part(2), part(3),
                  pl.BlockSpec(cst.shape, lambda i: (0, 0, 0, 0)),
                  pl.BlockSpec(gain.shape, lambda i: (0, 0))],
        out_specs=seq,
        out_shape=jax.ShapeDtypeStruct((b, s, RET_WIDTH), bf16),
        scratch_shapes=[
            pltpu.VMEM((s, RET_WIDTH), bf16),
            pltpu.VMEM((s // RET_CHUNK, RET_HEAD_DIM, RET_WIDTH), f32),
            pltpu.VMEM((s // RET_CHUNK, RET_HEAD_DIM, RET_WIDTH), bf16),
        ],
        compiler_params=pltpu.CompilerParams(
            dimension_semantics=("arbitrary",), vmem_limit_bytes=VMEM_LIMIT),
        name="retention",
    )(ret_act, ret_act, ret_act, ret_act, cst, gain)


def _out_mlp_kernel(x_ref, mf_ref, mr_ref, mod_ref, wo_ref, w1_ref, w2_ref, o_ref):
    gate_m = mod_ref[0, 2:3, :]
    shift_f = mod_ref[0, 3:4, :]
    scale_f = mod_ref[0, 4:5, :]
    gate_f = mod_ref[0, 5:6, :]
    d_ff = w1_ref.shape[1]
    fc = min(FF_CHUNK, d_ff)
    tm = x_ref.shape[1]
    sub = min(TOKEN_TILE, tm)
    for r in range(tm // sub):
        rs = slice(r * sub, (r + 1) * sub)
        mixed = _dot(mf_ref[0, rs, :], wo_ref[0:FOX_WIDTH, :]) + _dot(mr_ref[0, rs, :], wo_ref[FOX_WIDTH:, :])
        x1 = x_ref[0, rs, :] + gate_m * mixed
        inv = lax.rsqrt(jnp.mean(x1 * x1, axis=-1, keepdims=True) + EPS)
        h = (x1 * inv * (1.0 + scale_f) + shift_f).astype(bf16)
        y = jnp.zeros(x1.shape, f32)
        for j in range(d_ff // fc):
            u = jnp.maximum(_dot(h, w1_ref[:, j * fc:(j + 1) * fc]), 0.0)
            y = y + _dot((u * u).astype(bf16), w2_ref[j * fc:(j + 1) * fc, :])
        o_ref[0, rs, :] = x1 + gate_f * y


def _out_mlp(x, mixed_fox, mixed_ret, mod, w_out, w_mlp_in, w_mlp_out):
    b, s, d = x.shape
    tm = min(2 * TOKEN_TILE, s)
    wo = w_out.astype(bf16)
    w1 = w_mlp_in.astype(bf16)
    w2 = w_mlp_out.astype(bf16)
    tok = lambda width: pl.BlockSpec((1, tm, width), lambda i, j: (i, j, 0))
    resident = lambda a: pl.BlockSpec(a.shape, lambda i, j: (0, 0), pipeline_mode=pl.Buffered(1))
    return pl.pallas_call(
        _out_mlp_kernel,
        grid=(b, s // tm),
        in_specs=[
            tok(d), tok(FOX_WIDTH), tok(RET_WIDTH),
            pl.BlockSpec((1, N_MOD, d), lambda i, j: (i, 0, 0)),
            resident(wo), resident(w1), resident(w2),
        ],
        out_specs=tok(d),
        out_shape=jax.ShapeDtypeStruct((b, s, d), f32),
        compiler_params=pltpu.CompilerParams(
            dimension_semantics=("arbitrary", "arbitrary"), vmem_limit_bytes=VMEM_LIMIT),
        name="out_mlp",
    )(x, mixed_fox, mixed_ret, mod, wo, w1, w2)


def kernel(x, c, w_ada, b_ada, w_in, b_forget, q_norm_gain, k_norm_gain, fox_out_gain, ret_out_gain,
           w_out, w_mlp_in, w_mlp_out):
    b, s, d = x.shape
    assert d == FOX_WIDTH + RET_WIDTH and w_in.shape[-1] == 4 * FOX_WIDTH + FOX_HEADS + 4 * RET_WIDTH, (d, w_in.shape)
    assert s % min(2 * TOKEN_TILE, s) == 0 and s % min(ATTN_TILE, s) == 0 and s % RET_CHUNK == 0, s
    for l in range(w_ada.shape[0]):
        mod = _modulation(c, w_ada[l], b_ada[l]).reshape(b, N_MOD, d)
        fox_act, log_f, ret_act = _input_projection(
            x, mod, w_in[l], b_forget[l], q_norm_gain[l], k_norm_gain[l])
        mixed_fox = _fox_attention(fox_act, log_f, fox_out_gain[l])
        mixed_ret = _retention(ret_act, ret_out_gain[l])
        x = _out_mlp(x, mixed_fox, mixed_ret, mod, w_out[l], w_mlp_in[l], w_mlp_out[l])
    return x
```

```python
import numpy as np

import jax
import jax.numpy as jnp
from jax import lax
from jax.experimental import pallas as pl
from jax.experimental.pallas import tpu as pltpu

FOX_HEADS = 8
FOX_HEAD_DIM = 64
FOX_WIDTH = FOX_HEADS * FOX_HEAD_DIM
RET_HEADS = 4
RET_HEAD_DIM = 128
RET_WIDTH = RET_HEADS * RET_HEAD_DIM
RET_CHUNK = 128
ROPE_BASE = 10000.0
EPS = 1e-6
N_MOD = 6

LANES = 128
VMEM_LIMIT = 56 * 1024 * 1024
MASK_VALUE = -1e30
LOG2E = 1.4426950408889634

TOKEN_TILE = 512
ATTN_TILE = 256
FF_CHUNK = 1024

bf16 = jnp.bfloat16
f32 = jnp.float32


def _dot(a, b):
    return jnp.dot(a, b, preferred_element_type=f32)


def _dot_nt(a, b):
    return lax.dot_general(a, b, (((1,), (1,)), ((), ())), preferred_element_type=f32)


def _half_lane_mask():
    return lax.broadcasted_iota(jnp.int32, (1, LANES), 1) < FOX_HEAD_DIM


def _pair_sumsq(v, lo):
    sq = v * v
    s_lo = jnp.sum(jnp.where(lo, sq, 0.0), axis=-1, keepdims=True)
    s_hi = jnp.sum(jnp.where(lo, 0.0, sq), axis=-1, keepdims=True)
    return jnp.where(lo, s_lo, s_hi)


def _mod_kernel(c_ref, w_ref, b_ref, o_ref):
    c = c_ref[...]
    c_act = (c * jax.nn.sigmoid(c)).astype(bf16)
    o_ref[...] = _dot(c_act, w_ref[...].astype(bf16)) + b_ref[...]


def _modulation(c, w_ada, b_ada):
    b, d = c.shape
    n = w_ada.shape[1]
    return pl.pallas_call(
        _mod_kernel,
        grid=(n // d,),
        in_specs=[
            pl.BlockSpec((b, d), lambda j: (0, 0)),
            pl.BlockSpec((d, d), lambda j: (0, j)),
            pl.BlockSpec((1, d), lambda j: (0, j)),
        ],
        out_specs=pl.BlockSpec((b, d), lambda j: (0, j)),
        out_shape=jax.ShapeDtypeStruct((b, n), f32),
        compiler_params=pltpu.CompilerParams(
            dimension_semantics=("arbitrary",), vmem_limit_bytes=VMEM_LIMIT),
        name="adaln_mod",
    )(c, w_ada, b_ada.reshape(1, n))


def _inproj_kernel(x_ref, mod_ref, wf_ref, wg_ref, wr_ref, gq_ref, gk_ref, bf_ref, cos_ref, sin_ref,
                   fq_ref, fk_ref, fv_ref, fog_ref, lf_ref, rq_ref, rk_ref, rv_ref, rg_ref):
    lo = _half_lane_mask()
    w = FOX_WIDTH
    rw = RET_WIDTH
    tm = x_ref.shape[1]
    sub = min(TOKEN_TILE, tm)

    def qk_norm(p, gain_ref):
        outs = []
        for g in range(w // LANES):
            v = p[:, g * LANES:(g + 1) * LANES]
            ss = _pair_sumsq(v, lo)
            outs.append(v * lax.rsqrt(ss * (1.0 / FOX_HEAD_DIM) + EPS))
        return (jnp.concatenate(outs, axis=-1) * gain_ref[...]).astype(bf16)

    def rope(p, cos, sin, scale):
        outs = []
        for g in range(RET_HEADS):
            v = p[:, g * LANES:(g + 1) * LANES]
            outs.append(v * cos + pltpu.roll(v, RET_HEAD_DIM // 2, 1) * sin)
        r = jnp.concatenate(outs, axis=-1)
        return (r * scale if scale != 1.0 else r).astype(bf16)

    for r in range(tm // sub):
        rs = slice(r * sub, (r + 1) * sub)
        x = x_ref[0, rs, :]
        inv = lax.rsqrt(jnp.mean(x * x, axis=-1, keepdims=True) + EPS)
        h = (x * inv * (1.0 + mod_ref[0, 1:2, :]) + mod_ref[0, 0:1, :]).astype(bf16)

        def proj(w_ref, start, n):
            return _dot(h, w_ref[:, start:start + n])

        fq_ref[0, rs, :] = qk_norm(proj(wf_ref, 0, w), gq_ref)
        fk_ref[0, rs, :] = qk_norm(proj(wf_ref, w, w), gk_ref)
        fog_ref[0, rs, :] = jax.nn.sigmoid(proj(wf_ref, 3 * w, w)).astype(bf16)

        z = proj(wg_ref, 0, LANES) + bf_ref[...]
        log_f = jnp.minimum(z, 0.0) - jnp.log(1.0 + jnp.exp(-jnp.abs(z)))
        lf_ref[0, :, rs] = log_f.T[0:FOX_HEADS, :]

        cos = cos_ref[rs, :]
        sin = sin_ref[rs, :]
        rq_ref[0, rs, :] = rope(proj(wr_ref, 0, rw), cos, sin, 1.0)
        rk_ref[0, rs, :] = rope(proj(wr_ref, rw, rw), cos, sin, RET_HEAD_DIM ** -0.5)
        gate = proj(wr_ref, 3 * rw, rw)
        rg_ref[0, rs, :] = (gate * jax.nn.sigmoid(gate)).astype(bf16)
        rv_ref[0, rs, :] = proj(wr_ref, 2 * rw, rw).astype(bf16)
        fv_ref[0, rs, :] = proj(wf_ref, 2 * w, w).astype(bf16)


def _input_projection(x, mod, w_in, b_forget, q_gain, k_gain):
    b, s, d = x.shape
    tm = min(2 * TOKEN_TILE, s)
    o_ff = 4 * FOX_WIDTH
    o_r = o_ff + FOX_HEADS
    w_fox = w_in[:, :o_ff].astype(bf16)
    w_fg = jnp.pad(w_in[:, o_ff:o_r], ((0, 0), (0, LANES - FOX_HEADS))).astype(bf16)
    w_ret = w_in[:, o_r:].astype(bf16)
    bias_f = jnp.pad(b_forget, (0, LANES - FOX_HEADS)).reshape(1, LANES)
    gq = (jnp.tile(q_gain, FOX_HEADS) * (LOG2E * FOX_HEAD_DIM ** -0.5)).reshape(1, FOX_WIDTH)
    gk = jnp.tile(k_gain, FOX_HEADS).reshape(1, FOX_WIDTH)

    pos = np.arange(s, dtype=np.float32)
    inv_freq = np.float32(ROPE_BASE) ** (-np.arange(0, RET_HEAD_DIM, 2, dtype=np.float32) / np.float32(RET_HEAD_DIM))
    ang = pos[:, None] * inv_freq[None, :]
    cos_t = np.concatenate([np.cos(ang), np.cos(ang)], axis=-1).astype(np.float32)
    sin_t = np.concatenate([-np.sin(ang), np.sin(ang)], axis=-1).astype(np.float32)

    tok = lambda width: pl.BlockSpec((1, tm, width), lambda i, j: (i, j, 0))
    full = lambda a: pl.BlockSpec(a.shape, lambda i, j: (0,) * a.ndim, pipeline_mode=pl.Buffered(1))
    act = lambda width: jax.ShapeDtypeStruct((b, s, width), bf16)
    return pl.pallas_call(
        _inproj_kernel,
        grid=(b, s // tm),
        in_specs=[
            tok(d),
            pl.BlockSpec((1, N_MOD, d), lambda i, j: (i, 0, 0)),
            full(w_fox), full(w_fg), full(w_ret), full(gq), full(gk), full(bias_f),
            pl.BlockSpec((tm, LANES), lambda i, j: (j, 0)),
            pl.BlockSpec((tm, LANES), lambda i, j: (j, 0)),
        ],
        out_specs=[
            tok(FOX_WIDTH), tok(FOX_WIDTH), tok(FOX_WIDTH), tok(FOX_WIDTH),
            pl.BlockSpec((1, FOX_HEADS, tm), lambda i, j: (i, 0, j)),
            tok(RET_WIDTH), tok(RET_WIDTH), tok(RET_WIDTH), tok(RET_WIDTH),
        ],
        out_shape=[
            act(FOX_WIDTH), act(FOX_WIDTH), act(FOX_WIDTH), act(FOX_WIDTH),
            jax.ShapeDtypeStruct((b, FOX_HEADS, s), f32),
            act(RET_WIDTH), act(RET_WIDTH), act(RET_WIDTH), act(RET_WIDTH),
        ],
        compiler_params=pltpu.CompilerParams(
            dimension_semantics=("arbitrary", "arbitrary"), vmem_limit_bytes=VMEM_LIMIT),
        name="in_proj",
    )(x, mod, w_fox, w_fg, w_ret, gq, gk, bias_f, cos_t, sin_t)


def _lane_cumsum(x, out_ref):
    rows, s = x.shape
    r = lax.broadcasted_iota(jnp.int32, (LANES, LANES), 0)
    c = lax.broadcasted_iota(jnp.int32, (LANES, LANES), 1)
    tri = (r <= c).astype(bf16)
    hi = x.astype(bf16).astype(f32)
    rest = x - hi
    mid = rest.astype(bf16).astype(f32)
    low = (rest - mid).astype(bf16).astype(f32)
    n = s // LANES
    chunks = [slice(ch * LANES, (ch + 1) * LANES) for ch in range(n)]
    pieces = jnp.concatenate([g[:, sl] for g in (hi, mid, low) for sl in chunks], axis=0)
    rhs = jnp.concatenate([tri, jnp.ones((LANES, LANES), bf16)], axis=1)
    part = _dot(pieces.astype(bf16), rhs)
    offset = jnp.zeros((rows, LANES), f32)
    for ch, sl in enumerate(chunks):
        hi_c, mid_c, low_c = (part[(g * n + ch) * rows:(g * n + ch + 1) * rows] for g in range(3))
        both = hi_c + mid_c + low_c
        out_ref[:, sl] = both[:, 0:LANES] + offset
        offset = offset + both[:, LANES:]


def _fox_kernel(q_ref, k_ref, v_ref, lf_ref, og_ref, gain_ref, o_ref, cum_ref, dec_ref, s_ref):
    pair = pl.program_id(1)
    s = q_ref.shape[1]
    t = min(ATTN_TILE, s)

    @pl.when(pair == 0)
    def _():
        _lane_cumsum(lf_ref[0] * LOG2E, cum_ref)
        neg = -cum_ref[...]
        hi = neg.astype(bf16).astype(f32)
        mid = (neg - hi).astype(bf16).astype(f32)
        low = (neg - hi - mid).astype(bf16).astype(f32)
        sub = lax.broadcasted_iota(jnp.int32, (3 * FOX_HEADS, 1), 0)
        decay_t = jnp.zeros((3 * FOX_HEADS, s), f32)
        for h in range(FOX_HEADS):
            for g, piece in enumerate((hi, mid, low)):
                decay_t = jnp.where(sub == 3 * h + g, piece[h:h + 1], decay_t)
        decay_t = jnp.concatenate([decay_t, jnp.zeros((LANES - 3 * FOX_HEADS, s), f32)], axis=0)
        dec_ref[...] = decay_t.T.astype(bf16)

    lo = _half_lane_mask()
    row = lax.broadcasted_iota(jnp.int32, (t, t), 0)
    col = lax.broadcasted_iota(jnp.int32, (t, t), 1)
    causal = row >= col
    lane = lax.broadcasted_iota(jnp.int32, (1, LANES), 1)
    first = 6 * pair
    ones_a = jnp.where((lane >= first) & (lane < first + 3), 1.0, 0.0).astype(bf16)
    ones_b = jnp.where((lane >= first + 3) & (lane < first + 6), 1.0, 0.0).astype(bf16)
    nq = s // t
    row_max, pv_acc = {}, {}

    def score_pass(qi):
        q = q_ref[0, qi * t:(qi + 1) * t, :]
        zero = jnp.zeros_like(q)
        q2 = jnp.concatenate(
            [jnp.concatenate([jnp.where(lo, q, zero), jnp.broadcast_to(ones_a, q.shape)], axis=1),
             jnp.concatenate([jnp.where(lo, zero, q), jnp.broadcast_to(ones_b, q.shape)], axis=1)], axis=0)
        m_run = jnp.full((2 * t, LANES), MASK_VALUE, f32)
        for kb in range(qi + 1):
            ks = slice(kb * t, (kb + 1) * t)
            keys = jnp.concatenate([k_ref[0, ks, :], dec_ref[ks, :]], axis=1)
            sc = _dot_nt(q2, keys)
            if kb == qi:
                sc = jnp.concatenate([jnp.where(causal, sc[0:t], MASK_VALUE),
                                      jnp.where(causal, sc[t:], MASK_VALUE)], axis=0)
            s_ref[qi % 2, :, ks] = sc
            for j in range(t // LANES):
                m_run = jnp.maximum(m_run, sc[:, j * LANES:(j + 1) * LANES])
            yield
        row_max[qi] = jnp.max(m_run, axis=-1, keepdims=True)

    def prob_pass(qi):
        m = row_max.pop(qi)
        acc = None
        for kb in range(qi + 1):
            ks = slice(kb * t, (kb + 1) * t)
            p = jnp.exp2(s_ref[qi % 2, :, ks] - m).astype(bf16)
            v1 = jnp.concatenate([v_ref[0, ks, :], jnp.ones((t, LANES), bf16)], axis=1)
            part = _dot(p, v1)
            acc = part if acc is None else acc + part
            yield
        pv_acc[qi] = acc

    def finish(qi):
        rows = slice(qi * t, (qi + 1) * t)
        pv = pv_acc.pop(qi)
        o2 = pv[:, 0:LANES] / pv[:, LANES:]
        out = jnp.where(lo, o2[0:t], o2[t:])
        ss = _pair_sumsq(out, lo)
        out = out * lax.rsqrt(ss * (1.0 / FOX_HEAD_DIM) + EPS) * gain_ref[0]
        o_ref[0, rows, :] = (out * og_ref[0, rows, :].astype(f32)).astype(bf16)

    def interleave(*passes):
        live = list(passes)
        while live:
            live = [g for g in live if next(g, StopIteration) is not StopIteration]

    interleave(score_pass(nq - 1))
    for qi in reversed(range(nq)):
        if qi + 1 < nq:
            finish(qi + 1)
        interleave(*([score_pass(qi - 1)] if qi >= 1 else []), prob_pass(qi))
    finish(0)


def _fox_attention(fq, fk, fv, fog, log_f, out_gain):
    b, s, _ = fq.shape
    t = min(ATTN_TILE, s)
    pairs = FOX_WIDTH // LANES
    gain = out_gain.reshape(pairs, 1, LANES)
    seq = pl.BlockSpec((1, s, LANES), lambda i, p: (i, 0, p))
    return pl.pallas_call(
        _fox_kernel,
        grid=(b, pairs),
        in_specs=[
            seq, seq, seq,
            pl.BlockSpec((1, FOX_HEADS, s), lambda i, p: (i, 0, 0)),
            seq,
            pl.BlockSpec((1, 1, LANES), lambda i, p: (p, 0, 0)),
        ],
        out_specs=seq,
        out_shape=jax.ShapeDtypeStruct((b, s, FOX_WIDTH), bf16),
        scratch_shapes=[
            pltpu.VMEM((FOX_HEADS, s), f32),
            pltpu.VMEM((s, LANES), bf16),
            pltpu.VMEM((2, 2 * t, s), f32),
        ],
        compiler_params=pltpu.CompilerParams(
            dimension_semantics=("arbitrary", "arbitrary"), vmem_limit_bytes=VMEM_LIMIT),
        name="fox_attention",
    )(fq, fk, fv, log_f, fog, gain)


def _ret_kernel(q_ref, k_ref, v_ref, g_ref, cst_ref, gain_ref, o_ref, inner_ref, kv_ref, st_ref):
    s = q_ref.shape[1]
    c = RET_CHUNK
    heads = [slice(hd * RET_HEAD_DIM, (hd + 1) * RET_HEAD_DIM) for hd in range(RET_HEADS)]
    chunks = [slice(ci * c, (ci + 1) * c) for ci in range(s // c)]

    for hd, cs in enumerate(heads):
        for ci, rs in enumerate(chunks):
            k = k_ref[0, rs, cs]
            inner_ref[rs, cs] = (_dot_nt(q_ref[0, rs, cs], k) * cst_ref[hd, 0]).astype(bf16)
            kz = (k.astype(f32) * cst_ref[hd, 2]).T.astype(bf16)
            kv_ref[ci, :, cs] = _dot(kz, v_ref[0, rs, cs])

    for hd, cs in enumerate(heads):
        state = jnp.zeros((RET_HEAD_DIM, RET_HEAD_DIM), f32)
        for ci in range(len(chunks)):
            st_ref[ci, :, cs] = state.astype(bf16)
            state = state * cst_ref[hd, 3] + kv_ref[ci, :, cs]

    for hd, cs in enumerate(heads):
        for ci, rs in enumerate(chunks):
            q_xi = (q_ref[0, rs, cs].astype(f32) * cst_ref[hd, 1]).astype(bf16)
            lhs = jnp.concatenate([inner_ref[rs, cs], q_xi], axis=1)
            rhs = jnp.concatenate([v_ref[0, rs, cs], st_ref[ci, :, cs]], axis=0)
            out = _dot(lhs, rhs)
            inv = lax.rsqrt(jnp.mean(out * out, axis=-1, keepdims=True) + EPS)
            gate = g_ref[0, rs, cs].astype(f32)
            o_ref[0, rs, cs] = (out * inv * gain_ref[:, cs] * gate).astype(bf16)


def _retention_constants():
    c = RET_CHUNK
    f = np.float32
    log_g = np.log(f(1.0) - f(2.0) ** (f(-5.0) - np.arange(RET_HEADS, dtype=f)))
    n = np.arange(c, dtype=f)
    diff = n[:, None] - n[None, :]
    mask = np.where(diff[None] >= 0, np.exp(np.maximum(diff, f(0.0))[None] * log_g[:, None, None]), f(0.0))
    xi = np.exp((n[None, :] + f(1.0)) * log_g[:, None])
    zeta = np.exp((f(c) - f(1.0) - n[None, :]) * log_g[:, None])
    g_chunk = np.exp(f(c) * log_g)
    bc = lambda rows: np.broadcast_to(rows[:, :, None], (RET_HEADS, c, c))
    return np.stack([mask, bc(xi), bc(zeta), np.broadcast_to(g_chunk[:, None, None], (RET_HEADS, c, c))], axis=1).astype(f)


def _retention(rq, rk, rv, rg, out_gain):
    b, s, _ = rq.shape
    cst = _retention_constants()
    gain = out_gain.reshape(1, RET_WIDTH)
    seq = pl.BlockSpec((1, s, RET_WIDTH), lambda i: (i, 0, 0))
    return pl.pallas_call(
        _ret_kernel,
        grid=(b,),
        in_specs=[seq, seq, seq, seq,
                  pl.BlockSpec(cst.shape, lambda i: (0, 0, 0, 0)),
                  pl.BlockSpec(gain.shape, lambda i: (0, 0))],
        out_specs=seq,
        out_shape=jax.ShapeDtypeStruct((b, s, RET_WIDTH), bf16),
        scratch_shapes=[
            pltpu.VMEM((s, RET_WIDTH), bf16),
            pltpu.VMEM((s // RET_CHUNK, RET_HEAD_DIM, RET_WIDTH), f32),
            pltpu.VMEM((s // RET_CHUNK, RET_HEAD_DIM, RET_WIDTH), bf16),
        ],
        compiler_params=pltpu.CompilerParams(
            dimension_semantics=("arbitrary",), vmem_limit_bytes=VMEM_LIMIT),
        name="retention",
    )(rq, rk, rv, rg, cst, gain)


def _out_mlp_kernel(x_ref, mf_ref, mr_ref, mod_ref, wo_ref, w1_ref, w2_ref, o_ref):
    gate_m = mod_ref[0, 2:3, :]
    shift_f = mod_ref[0, 3:4, :]
    scale_f = mod_ref[0, 4:5, :]
    gate_f = mod_ref[0, 5:6, :]
    d_ff = w1_ref.shape[1]
    fc = min(FF_CHUNK, d_ff)
    tm = x_ref.shape[1]
    sub = min(TOKEN_TILE, tm)
    for r in range(tm // sub):
        rs = slice(r * sub, (r + 1) * sub)
        mixed = _dot(mf_ref[0, rs, :], wo_ref[0:FOX_WIDTH, :]) + _dot(mr_ref[0, rs, :], wo_ref[FOX_WIDTH:, :])
        x1 = x_ref[0, rs, :] + gate_m * mixed
        inv = lax.rsqrt(jnp.mean(x1 * x1, axis=-1, keepdims=True) + EPS)
        h = (x1 * inv * (1.0 + scale_f) + shift_f).astype(bf16)
        y = jnp.zeros(x1.shape, f32)
        for j in range(d_ff // fc):
            u = jnp.maximum(_dot(h, w1_ref[:, j * fc:(j + 1) * fc]), 0.0)
            y = y + _dot((u * u).astype(bf16), w2_ref[j * fc:(j + 1) * fc, :])
        o_ref[0, rs, :] = x1 + gate_f * y


def _out_mlp(x, mixed_fox, mixed_ret, mod, w_out, w_mlp_in, w_mlp_out):
    b, s, d = x.shape
    tm = min(2 * TOKEN_TILE, s)
    wo = w_out.astype(bf16)
    w1 = w_mlp_in.astype(bf16)
    w2 = w_mlp_out.astype(bf16)
    tok = lambda width: pl.BlockSpec((1, tm, width), lambda i, j: (i, j, 0))
    resident = lambda a: pl.BlockSpec(a.shape, lambda i, j: (0, 0), pipeline_mode=pl.Buffered(1))
    return pl.pallas_call(
        _out_mlp_kernel,
        grid=(b, s // tm),
        in_specs=[
            tok(d), tok(FOX_WIDTH), tok(RET_WIDTH),
            pl.BlockSpec((1, N_MOD, d), lambda i, j: (i, 0, 0)),
            resident(wo), resident(w1), resident(w2),
        ],
        out_specs=tok(d),
        out_shape=jax.ShapeDtypeStruct((b, s, d), f32),
        compiler_params=pltpu.CompilerParams(
            dimension_semantics=("arbitrary", "arbitrary"), vmem_limit_bytes=VMEM_LIMIT),
        name="out_mlp",
    )(x, mixed_fox, mixed_ret, mod, wo, w1, w2)


def kernel(x, c, w_ada, b_ada, w_in, b_forget, q_norm_gain, k_norm_gain, fox_out_gain, ret_out_gain,
           w_out, w_mlp_in, w_mlp_out):
    b, s, d = x.shape
    assert d == FOX_WIDTH + RET_WIDTH and w_in.shape[-1] == 4 * FOX_WIDTH + FOX_HEADS + 4 * RET_WIDTH, (d, w_in.shape)
    assert s % min(2 * TOKEN_TILE, s) == 0 and s % min(ATTN_TILE, s) == 0 and s % RET_CHUNK == 0, s
    assert w_mlp_in.shape[-1] % min(FF_CHUNK, w_mlp_in.shape[-1]) == 0, w_mlp_in.shape
    for l in range(w_ada.shape[0]):
        mod = _modulation(c, w_ada[l], b_ada[l]).reshape(b, N_MOD, d)
        fq, fk, fv, fog, log_f, rq, rk, rv, rg = _input_projection(
            x, mod, w_in[l], b_forget[l], q_norm_gain[l], k_norm_gain[l])
        mixed_fox = _fox_attention(fq, fk, fv, fog, log_f, fox_out_gain[l])
        mixed_ret = _retention(rq, rk, rv, rg, ret_out_gain[l])
        x = _out_mlp(x, mixed_fox, mixed_ret, mod, w_out[l], w_mlp_in[l], w_mlp_out[l])
    return x
```

```python
import numpy as np

import jax
import jax.numpy as jnp
from jax import lax
from jax.experimental import pallas as pl
from jax.experimental.pallas import tpu as pltpu

FOX_HEADS = 8
FOX_HEAD_DIM = 64
FOX_WIDTH = FOX_HEADS * FOX_HEAD_DIM
RET_HEADS = 4
RET_HEAD_DIM = 128
RET_WIDTH = RET_HEADS * RET_HEAD_DIM
RET_CHUNK = 128
ROPE_BASE = 10000.0
EPS = 1e-6
N_MOD = 6

LANES = 128
VMEM_LIMIT = 56 * 1024 * 1024
MASK_VALUE = -1e30
LOG2E = 1.4426950408889634

TOKEN_TILE = 512
ATTN_TILE = 256
FF_CHUNK = 1024

bf16 = jnp.bfloat16
f32 = jnp.float32


def _dot(a, b):
    return jnp.dot(a, b, preferred_element_type=f32)


def _dot_nt(a, b):
    return lax.dot_general(a, b, (((1,), (1,)), ((), ())), preferred_element_type=f32)


def _half_lane_mask():
    return lax.broadcasted_iota(jnp.int32, (1, LANES), 1) < FOX_HEAD_DIM


def _pair_sumsq(v, lo):
    sq = v * v
    s_lo = jnp.sum(jnp.where(lo, sq, 0.0), axis=-1, keepdims=True)
    s_hi = jnp.sum(jnp.where(lo, 0.0, sq), axis=-1, keepdims=True)
    return jnp.where(lo, s_lo, s_hi)


def _mod_kernel(c_ref, w_ref, b_ref, o_ref):
    c = c_ref[...]
    c_act = (c * jax.nn.sigmoid(c)).astype(bf16)
    o_ref[...] = _dot(c_act, w_ref[...].astype(bf16)) + b_ref[...]


def _modulation(c, w_ada, b_ada):
    b, d = c.shape
    n = w_ada.shape[1]
    return pl.pallas_call(
        _mod_kernel,
        grid=(n // d,),
        in_specs=[
            pl.BlockSpec((b, d), lambda j: (0, 0)),
            pl.BlockSpec((d, d), lambda j: (0, j)),
            pl.BlockSpec((1, d), lambda j: (0, j)),
        ],
        out_specs=pl.BlockSpec((b, d), lambda j: (0, j)),
        out_shape=jax.ShapeDtypeStruct((b, n), f32),
        compiler_params=pltpu.CompilerParams(
            dimension_semantics=("arbitrary",), vmem_limit_bytes=VMEM_LIMIT),
        name="adaln_mod",
    )(c, w_ada, b_ada.reshape(1, n))


def _inproj_kernel(x_ref, mod_ref, wf_ref, wg_ref, wr_ref, gq_ref, gk_ref, bf_ref, cos_ref, sin_ref,
                   fq_ref, fk_ref, fv_ref, fog_ref, lf_ref, rq_ref, rk_ref, rv_ref, rg_ref):
    lo = _half_lane_mask()
    w = FOX_WIDTH
    rw = RET_WIDTH
    tm = x_ref.shape[1]
    sub = min(TOKEN_TILE, tm)

    def qk_norm(p, gain_ref):
        outs = []
        for g in range(w // LANES):
            v = p[:, g * LANES:(g + 1) * LANES]
            ss = _pair_sumsq(v, lo)
            outs.append(v * lax.rsqrt(ss * (1.0 / FOX_HEAD_DIM) + EPS))
        return (jnp.concatenate(outs, axis=-1) * gain_ref[...]).astype(bf16)

    def rope(p, cos, sin, scale):
        outs = []
        for g in range(RET_HEADS):
            v = p[:, g * LANES:(g + 1) * LANES]
            outs.append(v * cos + pltpu.roll(v, RET_HEAD_DIM // 2, 1) * sin)
        r = jnp.concatenate(outs, axis=-1)
        return (r * scale if scale != 1.0 else r).astype(bf16)

    for r in range(tm // sub):
        rs = slice(r * sub, (r + 1) * sub)
        x = x_ref[0, rs, :]
        inv = lax.rsqrt(jnp.mean(x * x, axis=-1, keepdims=True) + EPS)
        h = (x * inv * (1.0 + mod_ref[0, 1:2, :]) + mod_ref[0, 0:1, :]).astype(bf16)

        def proj(w_ref, start, n):
            return _dot(h, w_ref[:, start:start + n])

        fq_ref[0, rs, :] = qk_norm(proj(wf_ref, 0, w), gq_ref)
        fk_ref[0, rs, :] = qk_norm(proj(wf_ref, w, w), gk_ref)
        fog_ref[0, rs, :] = jax.nn.sigmoid(proj(wf_ref, 3 * w, w)).astype(bf16)

        z = proj(wg_ref, 0, LANES) + bf_ref[...]
        log_f = jnp.minimum(z, 0.0) - jnp.log(1.0 + jnp.exp(-jnp.abs(z)))
        lf_ref[0, :, rs] = log_f.T[0:FOX_HEADS, :]

        cos = cos_ref[rs, :]
        sin = sin_ref[rs, :]
        rq_ref[0, rs, :] = rope(proj(wr_ref, 0, rw), cos, sin, 1.0)
        rk_ref[0, rs, :] = rope(proj(wr_ref, rw, rw), cos, sin, RET_HEAD_DIM ** -0.5)
        gate = proj(wr_ref, 3 * rw, rw)
        rg_ref[0, rs, :] = (gate * jax.nn.sigmoid(gate)).astype(bf16)
        rv_ref[0, rs, :] = proj(wr_ref, 2 * rw, rw).astype(bf16)
        fv_ref[0, rs, :] = proj(wf_ref, 2 * w, w).astype(bf16)


def _input_projection(x, mod, w_in, b_forget, q_gain, k_gain):
    b, s, d = x.shape
    tm = min(2 * TOKEN_TILE, s)
    o_ff = 4 * FOX_WIDTH
    o_r = o_ff + FOX_HEADS
    w_fox = w_in[:, :o_ff].astype(bf16)
    w_fg = jnp.pad(w_in[:, o_ff:o_r], ((0, 0), (0, LANES - FOX_HEADS))).astype(bf16)
    w_ret = w_in[:, o_r:].astype(bf16)
    bias_f = jnp.pad(b_forget, (0, LANES - FOX_HEADS)).reshape(1, LANES)
    gq = (jnp.tile(q_gain, FOX_HEADS) * (LOG2E * FOX_HEAD_DIM ** -0.5)).reshape(1, FOX_WIDTH)
    gk = jnp.tile(k_gain, FOX_HEADS).reshape(1, FOX_WIDTH)

    pos = np.arange(s, dtype=np.float32)
    inv_freq = np.float32(ROPE_BASE) ** (-np.arange(0, RET_HEAD_DIM, 2, dtype=np.float32) / np.float32(RET_HEAD_DIM))
    ang = pos[:, None] * inv_freq[None, :]
    cos_t = np.concatenate([np.cos(ang), np.cos(ang)], axis=-1).astype(np.float32)
    sin_t = np.concatenate([-np.sin(ang), np.sin(ang)], axis=-1).astype(np.float32)

    tok = lambda width: pl.BlockSpec((1, tm, width), lambda i, j: (i, j, 0))
    full = lambda a: pl.BlockSpec(a.shape, lambda i, j: (0,) * a.ndim, pipeline_mode=pl.Buffered(1))
    act = lambda width: jax.ShapeDtypeStruct((b, s, width), bf16)
    return pl.pallas_call(
        _inproj_kernel,
        grid=(b, s // tm),
        in_specs=[
            tok(d),
            pl.BlockSpec((1, N_MOD, d), lambda i, j: (i, 0, 0)),
            full(w_fox), full(w_fg), full(w_ret), full(gq), full(gk), full(bias_f),
            pl.BlockSpec((tm, LANES), lambda i, j: (j, 0)),
            pl.BlockSpec((tm, LANES), lambda i, j: (j, 0)),
        ],
        out_specs=[
            tok(FOX_WIDTH), tok(FOX_WIDTH), tok(FOX_WIDTH), tok(FOX_WIDTH),
            pl.BlockSpec((1, FOX_HEADS, tm), lambda i, j: (i, 0, j)),
            tok(RET_WIDTH), tok(RET_WIDTH), tok(RET_WIDTH), tok(RET_WIDTH),
        ],
        out_shape=[
            act(FOX_WIDTH), act(FOX_WIDTH), act(FOX_WIDTH), act(FOX_WIDTH),
            jax.ShapeDtypeStruct((b, FOX_HEADS, s), f32),
            act(RET_WIDTH), act(RET_WIDTH), act(RET_WIDTH), act(RET_WIDTH),
        ],
        compiler_params=pltpu.CompilerParams(
            dimension_semantics=("arbitrary", "arbitrary"), vmem_limit_bytes=VMEM_LIMIT),
        name="in_proj",
    )(x, mod, w_fox, w_fg, w_ret, gq, gk, bias_f, cos_t, sin_t)


def _lane_cumsum(x, out_ref):
    rows, s = x.shape
    r = lax.broadcasted_iota(jnp.int32, (LANES, LANES), 0)
    c = lax.broadcasted_iota(jnp.int32, (LANES, LANES), 1)
    tri = (r <= c).astype(bf16)
    hi = x.astype(bf16).astype(f32)
    rest = x - hi
    mid = rest.astype(bf16).astype(f32)
    low = (rest - mid).astype(bf16).astype(f32)
    n = s // LANES
    chunks = [slice(ch * LANES, (ch + 1) * LANES) for ch in range(n)]
    pieces = jnp.concatenate([g[:, sl] for g in (hi, mid, low) for sl in chunks], axis=0)
    rhs = jnp.concatenate([tri, jnp.ones((LANES, LANES), bf16)], axis=1)
    part = _dot(pieces.astype(bf16), rhs)
    offset = jnp.zeros((rows, LANES), f32)
    for ch, sl in enumerate(chunks):
        hi_c, mid_c, low_c = (part[(g * n + ch) * rows:(g * n + ch + 1) * rows] for g in range(3))
        both = hi_c + mid_c + low_c
        out_ref[:, sl] = both[:, 0:LANES] + offset
        offset = offset + both[:, LANES:]


def _fox_kernel(q_ref, k_ref, v_ref, lf_ref, o_ref, cum_ref, dec_ref, s_ref):
    pair = pl.program_id(1)
    s = q_ref.shape[1]
    t = min(ATTN_TILE, s)

    @pl.when(pair == 0)
    def _():
        _lane_cumsum(lf_ref[0] * LOG2E, cum_ref)
        neg = -cum_ref[...]
        hi = neg.astype(bf16).astype(f32)
        mid = (neg - hi).astype(bf16).astype(f32)
        low = (neg - hi - mid).astype(bf16).astype(f32)
        sub = lax.broadcasted_iota(jnp.int32, (3 * FOX_HEADS, 1), 0)
        decay_t = jnp.zeros((3 * FOX_HEADS, s), f32)
        for h in range(FOX_HEADS):
            for g, piece in enumerate((hi, mid, low)):
                decay_t = jnp.where(sub == 3 * h + g, piece[h:h + 1], decay_t)
        decay_t = jnp.concatenate([decay_t, jnp.zeros((LANES - 3 * FOX_HEADS, s), f32)], axis=0)
        dec_ref[...] = decay_t.T.astype(bf16)

    lo = _half_lane_mask()
    row = lax.broadcasted_iota(jnp.int32, (t, t), 0)
    col = lax.broadcasted_iota(jnp.int32, (t, t), 1)
    causal = row >= col
    lane = lax.broadcasted_iota(jnp.int32, (1, LANES), 1)
    first = 6 * pair
    ones_a = jnp.where((lane >= first) & (lane < first + 3), 1.0, 0.0).astype(bf16)
    ones_b = jnp.where((lane >= first + 3) & (lane < first + 6), 1.0, 0.0).astype(bf16)
    nq = s // t
    row_max, pv_acc = {}, {}

    def score_pass(qi):
        q = q_ref[0, qi * t:(qi + 1) * t, :]
        zero = jnp.zeros_like(q)
        q2 = jnp.concatenate(
            [jnp.concatenate([jnp.where(lo, q, zero), jnp.broadcast_to(ones_a, q.shape)], axis=1),
             jnp.concatenate([jnp.where(lo, zero, q), jnp.broadcast_to(ones_b, q.shape)], axis=1)], axis=0)
        m_run = jnp.full((2 * t, LANES), MASK_VALUE, f32)
        for kb in range(qi + 1):
            ks = slice(kb * t, (kb + 1) * t)
            keys = jnp.concatenate([k_ref[0, ks, :], dec_ref[ks, :]], axis=1)
            sc = _dot_nt(q2, keys)
            if kb == qi:
                sc = jnp.concatenate([jnp.where(causal, sc[0:t], MASK_VALUE),
                                      jnp.where(causal, sc[t:], MASK_VALUE)], axis=0)
            s_ref[qi % 2, :, ks] = sc
            for j in range(t // LANES):
                m_run = jnp.maximum(m_run, sc[:, j * LANES:(j + 1) * LANES])
            yield
        row_max[qi] = jnp.max(m_run, axis=-1, keepdims=True)

    def prob_pass(qi):
        m = row_max.pop(qi)
        acc = None
        for kb in range(qi + 1):
            ks = slice(kb * t, (kb + 1) * t)
            p = jnp.exp2(s_ref[qi % 2, :, ks] - m).astype(bf16)
            v1 = jnp.concatenate([v_ref[0, ks, :], jnp.ones((t, LANES), bf16)], axis=1)
            part = _dot(p, v1)
            acc = part if acc is None else acc + part
            yield
        pv_acc[qi] = acc

    def finish(qi):
        rows = slice(qi * t, (qi + 1) * t)
        pv = pv_acc.pop(qi)
        o2 = pv[:, 0:LANES] / pv[:, LANES:]
        o_ref[0, rows, :] = jnp.where(lo, o2[0:t], o2[t:]).astype(bf16)

    def interleave(*passes):
        live = list(passes)
        while live:
            live = [g for g in live if next(g, StopIteration) is not StopIteration]

    interleave(score_pass(nq - 1))
    for qi in reversed(range(nq)):
        if qi + 1 < nq:
            finish(qi + 1)
        interleave(*([score_pass(qi - 1)] if qi >= 1 else []), prob_pass(qi))
    finish(0)


def _fox_attention(fq, fk, fv, log_f):
    b, s, _ = fq.shape
    t = min(ATTN_TILE, s)
    pairs = FOX_WIDTH // LANES
    seq = pl.BlockSpec((1, s, LANES), lambda i, p: (i, 0, p))
    return pl.pallas_call(
        _fox_kernel,
        grid=(b, pairs),
        in_specs=[
            seq, seq, seq,
            pl.BlockSpec((1, FOX_HEADS, s), lambda i, p: (i, 0, 0)),
        ],
        out_specs=seq,
        out_shape=jax.ShapeDtypeStruct((b, s, FOX_WIDTH), bf16),
        scratch_shapes=[
            pltpu.VMEM((FOX_HEADS, s), f32),
            pltpu.VMEM((s, LANES), bf16),
            pltpu.VMEM((2, 2 * t, s), f32),
        ],
        compiler_params=pltpu.CompilerParams(
            dimension_semantics=("arbitrary", "arbitrary"), vmem_limit_bytes=VMEM_LIMIT),
        name="fox_attention",
    )(fq, fk, fv, log_f)


def _ret_kernel(q_ref, k_ref, v_ref, g_ref, cst_ref, gain_ref, o_ref, inner_ref, kv_ref, st_ref):
    s = q_ref.shape[1]
    c = RET_CHUNK
    heads = [slice(hd * RET_HEAD_DIM, (hd + 1) * RET_HEAD_DIM) for hd in range(RET_HEADS)]
    chunks = [slice(ci * c, (ci + 1) * c) for ci in range(s // c)]

    for hd, cs in enumerate(heads):
        for ci, rs in enumerate(chunks):
            k = k_ref[0, rs, cs]
            inner_ref[rs, cs] = (_dot_nt(q_ref[0, rs, cs], k) * cst_ref[hd, 0]).astype(bf16)
            kz = (k.astype(f32) * cst_ref[hd, 2]).T.astype(bf16)
            kv_ref[ci, :, cs] = _dot(kz, v_ref[0, rs, cs])

    for hd, cs in enumerate(heads):
        state = jnp.zeros((RET_HEAD_DIM, RET_HEAD_DIM), f32)
        for ci in range(len(chunks)):
            st_ref[ci, :, cs] = state.astype(bf16)
            state = state * cst_ref[hd, 3] + kv_ref[ci, :, cs]

    for hd, cs in enumerate(heads):
        for ci, rs in enumerate(chunks):
            q_xi = (q_ref[0, rs, cs].astype(f32) * cst_ref[hd, 1]).astype(bf16)
            lhs = jnp.concatenate([inner_ref[rs, cs], q_xi], axis=1)
            rhs = jnp.concatenate([v_ref[0, rs, cs], st_ref[ci, :, cs]], axis=0)
            out = _dot(lhs, rhs)
            inv = lax.rsqrt(jnp.mean(out * out, axis=-1, keepdims=True) + EPS)
            gate = g_ref[0, rs, cs].astype(f32)
            o_ref[0, rs, cs] = (out * inv * gain_ref[:, cs] * gate).astype(bf16)


def _retention_constants():
    c = RET_CHUNK
    f = np.float32
    log_g = np.log(f(1.0) - f(2.0) ** (f(-5.0) - np.arange(RET_HEADS, dtype=f)))
    n = np.arange(c, dtype=f)
    diff = n[:, None] - n[None, :]
    mask = np.where(diff[None] >= 0, np.exp(np.maximum(diff, f(0.0))[None] * log_g[:, None, None]), f(0.0))
    xi = np.exp((n[None, :] + f(1.0)) * log_g[:, None])
    zeta = np.exp((f(c) - f(1.0) - n[None, :]) * log_g[:, None])
    g_chunk = np.exp(f(c) * log_g)
    bc = lambda rows: np.broadcast_to(rows[:, :, None], (RET_HEADS, c, c))
    return np.stack([mask, bc(xi), bc(zeta), np.broadcast_to(g_chunk[:, None, None], (RET_HEADS, c, c))], axis=1).astype(f)


def _retention(rq, rk, rv, rg, out_gain):
    b, s, _ = rq.shape
    cst = _retention_constants()
    gain = out_gain.reshape(1, RET_WIDTH)
    seq = pl.BlockSpec((1, s, RET_WIDTH), lambda i: (i, 0, 0))
    return pl.pallas_call(
        _ret_kernel,
        grid=(b,),
        in_specs=[seq, seq, seq, seq,
                  pl.BlockSpec(cst.shape, lambda i: (0, 0, 0, 0)),
                  pl.BlockSpec(gain.shape, lambda i: (0, 0))],
        out_specs=seq,
        out_shape=jax.ShapeDtypeStruct((b, s, RET_WIDTH), bf16),
        scratch_shapes=[
            pltpu.VMEM((s, RET_WIDTH), bf16),
            pltpu.VMEM((s // RET_CHUNK, RET_HEAD_DIM, RET_WIDTH), f32),
            pltpu.VMEM((s // RET_CHUNK, RET_HEAD_DIM, RET_WIDTH), bf16),
        ],
        compiler_params=pltpu.CompilerParams(
            dimension_semantics=("arbitrary",), vmem_limit_bytes=VMEM_LIMIT),
        name="retention",
    )(rq, rk, rv, rg, cst, gain)


def _out_mlp_kernel(x_ref, af_ref, og_ref, mr_ref, mod_ref, fg_ref, wo_ref, w1_ref, w2_ref, o_ref):
    gate_m = mod_ref[0, 2:3, :]
    shift_f = mod_ref[0, 3:4, :]
    scale_f = mod_ref[0, 4:5, :]
    gate_f = mod_ref[0, 5:6, :]
    d_ff = w1_ref.shape[1]
    fc = min(FF_CHUNK, d_ff)
    lo = _half_lane_mask()
    tm = x_ref.shape[1]
    sub = min(TOKEN_TILE, tm)
    for r in range(tm // sub):
        rs = slice(r * sub, (r + 1) * sub)
        att = af_ref[0, rs, :].astype(f32)
        normed = []
        for g in range(FOX_WIDTH // LANES):
            v = att[:, g * LANES:(g + 1) * LANES]
            normed.append(v * lax.rsqrt(_pair_sumsq(v, lo) * (1.0 / FOX_HEAD_DIM) + EPS))
        mixed_fox = (jnp.concatenate(normed, axis=-1) * fg_ref[...] * og_ref[0, rs, :].astype(f32)).astype(bf16)
        mixed = _dot(mixed_fox, wo_ref[0:FOX_WIDTH, :]) + _dot(mr_ref[0, rs, :], wo_ref[FOX_WIDTH:, :])
        x1 = x_ref[0, rs, :] + gate_m * mixed
        inv = lax.rsqrt(jnp.mean(x1 * x1, axis=-1, keepdims=True) + EPS)
        h = (x1 * inv * (1.0 + scale_f) + shift_f).astype(bf16)
        y = jnp.zeros(x1.shape, f32)
        for j in range(d_ff // fc):
            u = jnp.maximum(_dot(h, w1_ref[:, j * fc:(j + 1) * fc]), 0.0)
            y = y + _dot((u * u).astype(bf16), w2_ref[j * fc:(j + 1) * fc, :])
        o_ref[0, rs, :] = x1 + gate_f * y


def _out_mlp(x, attn_fox, fog, mixed_ret, mod, fox_gain, w_out, w_mlp_in, w_mlp_out):
    b, s, d = x.shape
    tm = min(2 * TOKEN_TILE, s)
    wo = w_out.astype(bf16)
    w1 = w_mlp_in.astype(bf16)
    w2 = w_mlp_out.astype(bf16)
    fg = fox_gain.reshape(1, FOX_WIDTH)
    tok = lambda width: pl.BlockSpec((1, tm, width), lambda i, j: (i, j, 0))
    resident = lambda a: pl.BlockSpec(a.shape, lambda i, j: (0, 0), pipeline_mode=pl.Buffered(1))
    return pl.pallas_call(
        _out_mlp_kernel,
        grid=(b, s // tm),
        in_specs=[
            tok(d), tok(FOX_WIDTH), tok(FOX_WIDTH), tok(RET_WIDTH),
            pl.BlockSpec((1, N_MOD, d), lambda i, j: (i, 0, 0)),
            resident(fg), resident(wo), resident(w1), resident(w2),
        ],
        out_specs=tok(d),
        out_shape=jax.ShapeDtypeStruct((b, s, d), f32),
        compiler_params=pltpu.CompilerParams(
            dimension_semantics=("arbitrary", "arbitrary"), vmem_limit_bytes=VMEM_LIMIT),
        name="out_mlp",
    )(x, attn_fox, fog, mixed_ret, mod, fg, wo, w1, w2)


def kernel(x, c, w_ada, b_ada, w_in, b_forget, q_norm_gain, k_norm_gain, fox_out_gain, ret_out_gain,
           w_out, w_mlp_in, w_mlp_out):
    b, s, d = x.shape
    for l in range(w_ada.shape[0]):
        mod = _modulation(c, w_ada[l], b_ada[l]).reshape(b, N_MOD, d)
        fq, fk, fv, fog, log_f, rq, rk, rv, rg = _input_projection(
            x, mod, w_in[l], b_forget[l], q_norm_gain[l], k_norm_gain[l])
        attn_fox = _fox_attention(fq, fk, fv, log_f)
        mixed_ret = _retention(rq, rk, rv, rg, ret_out_gain[l])
        x = _out_mlp(x, attn_fox, fog, mixed_ret, mod, fox_out_gain[l], w_out[l], w_mlp_in[l], w_mlp_out[l])
    return x
```

```python
import numpy as np

import jax
import jax.numpy as jnp
from jax import lax
from jax.experimental import pallas as pl
from jax.experimental.pallas import tpu as pltpu

FOX_HEADS = 8
FOX_HEAD_DIM = 64
FOX_WIDTH = FOX_HEADS * FOX_HEAD_DIM
RET_HEADS = 4
RET_HEAD_DIM = 128
RET_WIDTH = RET_HEADS * RET_HEAD_DIM
RET_CHUNK = 128
ROPE_BASE = 10000.0
EPS = 1e-6
N_MOD = 6

LANES = 128
VMEM_LIMIT = 56 * 1024 * 1024
MASK_VALUE = -1e30
LOG2E = 1.4426950408889634

TOKEN_TILE = 512
ATTN_TILE = 256
FF_CHUNK = 1024

bf16 = jnp.bfloat16
f32 = jnp.float32


def _dot(a, b):
    return jnp.dot(a, b, preferred_element_type=f32)


def _dot_nt(a, b):
    return lax.dot_general(a, b, (((1,), (1,)), ((), ())), preferred_element_type=f32)


def _half_lane_mask():
    return lax.broadcasted_iota(jnp.int32, (1, LANES), 1) < FOX_HEAD_DIM


def _pair_sumsq(v, lo):
    sq = v * v
    s_lo = jnp.sum(jnp.where(lo, sq, 0.0), axis=-1, keepdims=True)
    s_hi = jnp.sum(jnp.where(lo, 0.0, sq), axis=-1, keepdims=True)
    return jnp.where(lo, s_lo, s_hi)


def _mod_kernel(c_ref, w_ref, b_ref, o_ref):
    c = c_ref[...]
    c_act = (c * jax.nn.sigmoid(c)).astype(bf16)
    o_ref[...] = _dot(c_act, w_ref[...].astype(bf16)) + b_ref[...]


def _modulation(c, w_ada, b_ada):
    b, d = c.shape
    n = w_ada.shape[1]
    return pl.pallas_call(
        _mod_kernel,
        grid=(n // d,),
        in_specs=[
            pl.BlockSpec((b, d), lambda j: (0, 0)),
            pl.BlockSpec((d, d), lambda j: (0, j)),
            pl.BlockSpec((1, d), lambda j: (0, j)),
        ],
        out_specs=pl.BlockSpec((b, d), lambda j: (0, j)),
        out_shape=jax.ShapeDtypeStruct((b, n), f32),
        compiler_params=pltpu.CompilerParams(
            dimension_semantics=("arbitrary",), vmem_limit_bytes=VMEM_LIMIT),
        name="adaln_mod",
    )(c, w_ada, b_ada.reshape(1, n))


def _inproj_kernel(x_ref, mod_ref, wf_ref, wg_ref, wr_ref, gq_ref, gk_ref, bf_ref, cos_ref, sin_ref,
                   fq_ref, fk_ref, fv_ref, fog_ref, lf_ref, rq_ref, rk_ref, rv_ref, rg_ref):
    lo = _half_lane_mask()
    w = FOX_WIDTH
    rw = RET_WIDTH
    tm = x_ref.shape[1]
    sub = min(TOKEN_TILE, tm)

    def qk_norm(p, gain_ref):
        outs = []
        for g in range(w // LANES):
            v = p[:, g * LANES:(g + 1) * LANES]
            ss = _pair_sumsq(v, lo)
            outs.append(v * lax.rsqrt(ss * (1.0 / FOX_HEAD_DIM) + EPS))
        return (jnp.concatenate(outs, axis=-1) * gain_ref[...]).astype(bf16)

    def rope(p, cos, sin, scale):
        outs = []
        for g in range(RET_HEADS):
            v = p[:, g * LANES:(g + 1) * LANES]
            outs.append(v * cos + pltpu.roll(v, RET_HEAD_DIM // 2, 1) * sin)
        r = jnp.concatenate(outs, axis=-1)
        return (r * scale if scale != 1.0 else r).astype(bf16)

    for r in range(tm // sub):
        rs = slice(r * sub, (r + 1) * sub)
        x = x_ref[0, rs, :]
        inv = lax.rsqrt(jnp.mean(x * x, axis=-1, keepdims=True) + EPS)
        h = (x * inv * (1.0 + mod_ref[0, 1:2, :]) + mod_ref[0, 0:1, :]).astype(bf16)

        def proj(w_ref, start, n):
            return _dot(h, w_ref[:, start:start + n])

        fq_ref[0, rs, :] = qk_norm(proj(wf_ref, 0, w), gq_ref)
        fk_ref[0, rs, :] = qk_norm(proj(wf_ref, w, w), gk_ref)
        fog_ref[0, rs, :] = jax.nn.sigmoid(proj(wf_ref, 3 * w, w)).astype(bf16)

        z = proj(wg_ref, 0, LANES) + bf_ref[...]
        log_f = jnp.minimum(z, 0.0) - jnp.log(1.0 + jnp.exp(-jnp.abs(z)))
        lf_ref[0, :, rs] = log_f.T[0:FOX_HEADS, :]

        cos = cos_ref[rs, :]
        sin = sin_ref[rs, :]
        rq_ref[0, rs, :] = rope(proj(wr_ref, 0, rw), cos, sin, 1.0)
        rk_ref[0, rs, :] = rope(proj(wr_ref, rw, rw), cos, sin, RET_HEAD_DIM ** -0.5)
        gate = proj(wr_ref, 3 * rw, rw)
        rg_ref[0, rs, :] = (gate * jax.nn.sigmoid(gate)).astype(bf16)
        rv_ref[0, rs, :] = proj(wr_ref, 2 * rw, rw).astype(bf16)
        fv_ref[0, rs, :] = proj(wf_ref, 2 * w, w).astype(bf16)


def _input_projection(x, mod, w_in, b_forget, q_gain, k_gain):
    b, s, d = x.shape
    tm = min(2 * TOKEN_TILE, s)
    o_ff = 4 * FOX_WIDTH
    o_r = o_ff + FOX_HEADS
    w_fox = w_in[:, :o_ff].astype(bf16)
    w_fg = jnp.pad(w_in[:, o_ff:o_r], ((0, 0), (0, LANES - FOX_HEADS))).astype(bf16)
    w_ret = w_in[:, o_r:].astype(bf16)
    bias_f = jnp.pad(b_forget, (0, LANES - FOX_HEADS)).reshape(1, LANES)
    gq = (jnp.tile(q_gain, FOX_HEADS) * (LOG2E * FOX_HEAD_DIM ** -0.5)).reshape(1, FOX_WIDTH)
    gk = jnp.tile(k_gain, FOX_HEADS).reshape(1, FOX_WIDTH)

    pos = np.arange(s, dtype=np.float32)
    inv_freq = np.float32(ROPE_BASE) ** (-np.arange(0, RET_HEAD_DIM, 2, dtype=np.float32) / np.float32(RET_HEAD_DIM))
    ang = pos[:, None] * inv_freq[None, :]
    cos_t = np.concatenate([np.cos(ang), np.cos(ang)], axis=-1).astype(np.float32)
    sin_t = np.concatenate([-np.sin(ang), np.sin(ang)], axis=-1).astype(np.float32)

    tok = lambda width: pl.BlockSpec((1, tm, width), lambda i, j: (i, j, 0))
    full = lambda a: pl.BlockSpec(a.shape, lambda i, j: (0,) * a.ndim, pipeline_mode=pl.Buffered(1))
    act = lambda width: jax.ShapeDtypeStruct((b, s, width), bf16)
    return pl.pallas_call(
        _inproj_kernel,
        grid=(b, s // tm),
        in_specs=[
            tok(d),
            pl.BlockSpec((1, N_MOD, d), lambda i, j: (i, 0, 0)),
            full(w_fox), full(w_fg), full(w_ret), full(gq), full(gk), full(bias_f),
            pl.BlockSpec((tm, LANES), lambda i, j: (j, 0)),
            pl.BlockSpec((tm, LANES), lambda i, j: (j, 0)),
        ],
        out_specs=[
            tok(FOX_WIDTH), tok(FOX_WIDTH), tok(FOX_WIDTH), tok(FOX_WIDTH),
            pl.BlockSpec((1, FOX_HEADS, tm), lambda i, j: (i, 0, j)),
            tok(RET_WIDTH), tok(RET_WIDTH), tok(RET_WIDTH), tok(RET_WIDTH),
        ],
        out_shape=[
            act(FOX_WIDTH), act(FOX_WIDTH), act(FOX_WIDTH), act(FOX_WIDTH),
            jax.ShapeDtypeStruct((b, FOX_HEADS, s), f32),
            act(RET_WIDTH), act(RET_WIDTH), act(RET_WIDTH), act(RET_WIDTH),
        ],
        compiler_params=pltpu.CompilerParams(
            dimension_semantics=("arbitrary", "arbitrary"), vmem_limit_bytes=VMEM_LIMIT),
        name="in_proj",
    )(x, mod, w_fox, w_fg, w_ret, gq, gk, bias_f, cos_t, sin_t)


def _lane_cumsum(x, out_ref):
    rows, s = x.shape
    r = lax.broadcasted_iota(jnp.int32, (LANES, LANES), 0)
    c = lax.broadcasted_iota(jnp.int32, (LANES, LANES), 1)
    tri = (r <= c).astype(bf16)
    hi = x.astype(bf16).astype(f32)
    rest = x - hi
    mid = rest.astype(bf16).astype(f32)
    low = (rest - mid).astype(bf16).astype(f32)
    n = s // LANES
    chunks = [slice(ch * LANES, (ch + 1) * LANES) for ch in range(n)]
    pieces = jnp.concatenate([g[:, sl] for g in (hi, mid, low) for sl in chunks], axis=0)
    rhs = jnp.concatenate([tri, jnp.ones((LANES, LANES), bf16)], axis=1)
    part = _dot(pieces.astype(bf16), rhs)
    offset = jnp.zeros((rows, LANES), f32)
    for ch, sl in enumerate(chunks):
        hi_c, mid_c, low_c = (part[(g * n + ch) * rows:(g * n + ch + 1) * rows] for g in range(3))
        both = hi_c + mid_c + low_c
        out_ref[:, sl] = both[:, 0:LANES] + offset
        offset = offset + both[:, LANES:]


def _fox_kernel(q_ref, k_ref, v_ref, lf_ref, o_ref, cum_ref, dec_ref, s_ref):
    pair = pl.program_id(1)
    s = q_ref.shape[1]
    t = min(ATTN_TILE, s)

    @pl.when(pair == 0)
    def _():
        _lane_cumsum(lf_ref[0] * LOG2E, cum_ref)
        neg = -cum_ref[...]
        hi = neg.astype(bf16).astype(f32)
        mid = (neg - hi).astype(bf16).astype(f32)
        low = (neg - hi - mid).astype(bf16).astype(f32)
        sub = lax.broadcasted_iota(jnp.int32, (3 * FOX_HEADS, 1), 0)
        decay_t = jnp.zeros((3 * FOX_HEADS, s), f32)
        for h in range(FOX_HEADS):
            for g, piece in enumerate((hi, mid, low)):
                decay_t = jnp.where(sub == 3 * h + g, piece[h:h + 1], decay_t)
        decay_t = jnp.concatenate([decay_t, jnp.zeros((LANES - 3 * FOX_HEADS, s), f32)], axis=0)
        dec_ref[...] = decay_t.T.astype(bf16)

    lo = _half_lane_mask()
    row = lax.broadcasted_iota(jnp.int32, (t, t), 0)
    col = lax.broadcasted_iota(jnp.int32, (t, t), 1)
    causal = row >= col
    lane = lax.broadcasted_iota(jnp.int32, (1, LANES), 1)
    first = 6 * pair
    ones_a = jnp.where((lane >= first) & (lane < first + 3), 1.0, 0.0).astype(bf16)
    ones_b = jnp.where((lane >= first + 3) & (lane < first + 6), 1.0, 0.0).astype(bf16)
    nq = s // t
    row_max, pv_acc = {}, {}

    def score_pass(qi):
        q = q_ref[0, qi * t:(qi + 1) * t, :]
        zero = jnp.zeros_like(q)
        q2 = jnp.concatenate(
            [jnp.concatenate([jnp.where(lo, q, zero), jnp.broadcast_to(ones_a, q.shape)], axis=1),
             jnp.concatenate([jnp.where(lo, zero, q), jnp.broadcast_to(ones_b, q.shape)], axis=1)], axis=0)
        m_run = jnp.full((2 * t, LANES), MASK_VALUE, f32)
        for kb in range(qi + 1):
            ks = slice(kb * t, (kb + 1) * t)
            keys = jnp.concatenate([k_ref[0, ks, :], dec_ref[ks, :]], axis=1)
            sc = _dot_nt(q2, keys)
            if kb == qi:
                sc = jnp.concatenate([jnp.where(causal, sc[0:t], MASK_VALUE),
                                      jnp.where(causal, sc[t:], MASK_VALUE)], axis=0)
            s_ref[qi % 2, :, ks] = sc
            for j in range(t // LANES):
                m_run = jnp.maximum(m_run, sc[:, j * LANES:(j + 1) * LANES])
            yield
        row_max[qi] = jnp.max(m_run, axis=-1, keepdims=True)

    def prob_pass(qi):
        m = row_max.pop(qi)
        acc = None
        for kb in range(qi + 1):
            ks = slice(kb * t, (kb + 1) * t)
            p = jnp.exp2(s_ref[qi % 2, :, ks] - m).astype(bf16)
            v1 = jnp.concatenate([v_ref[0, ks, :], jnp.ones((t, LANES), bf16)], axis=1)
            part = _dot(p, v1)
            acc = part if acc is None else acc + part
            yield
        pv_acc[qi] = acc

    def finish(qi):
        rows = slice(qi * t, (qi + 1) * t)
        pv = pv_acc.pop(qi)
        o2 = pv[:, 0:LANES] / pv[:, LANES:]
        o_ref[0, rows, :] = jnp.where(lo, o2[0:t], o2[t:]).astype(bf16)

    def interleave(*passes):
        live = list(passes)
        while live:
            live = [g for g in live if next(g, StopIteration) is not StopIteration]

    interleave(score_pass(nq - 1))
    for qi in reversed(range(nq)):
        if qi + 1 < nq:
            finish(qi + 1)
        interleave(*([score_pass(qi - 1)] if qi >= 1 else []), prob_pass(qi))
    finish(0)


def _fox_attention(fq, fk, fv, log_f):
    b, s, _ = fq.shape
    t = min(ATTN_TILE, s)
    pairs = FOX_WIDTH // LANES
    seq = pl.BlockSpec((1, s, LANES), lambda i, p: (i, 0, p))
    return pl.pallas_call(
        _fox_kernel,
        grid=(b, pairs),
        in_specs=[
            seq, seq, seq,
            pl.BlockSpec((1, FOX_HEADS, s), lambda i, p: (i, 0, 0)),
        ],
        out_specs=seq,
        out_shape=jax.ShapeDtypeStruct((b, s, FOX_WIDTH), bf16),
        scratch_shapes=[
            pltpu.VMEM((FOX_HEADS, s), f32),
            pltpu.VMEM((s, LANES), bf16),
            pltpu.VMEM((2, 2 * t, s), f32),
        ],
        compiler_params=pltpu.CompilerParams(
            dimension_semantics=("arbitrary", "arbitrary"), vmem_limit_bytes=VMEM_LIMIT),
        name="fox_attention",
    )(fq, fk, fv, log_f)


def _ret_kernel(q_ref, k_ref, v_ref, cst_ref, o_ref, inner_ref, kv_ref, st_ref):
    s = q_ref.shape[1]
    c = RET_CHUNK
    heads = [slice(hd * RET_HEAD_DIM, (hd + 1) * RET_HEAD_DIM) for hd in range(RET_HEADS)]
    chunks = [slice(ci * c, (ci + 1) * c) for ci in range(s // c)]

    for hd, cs in enumerate(heads):
        for ci, rs in enumerate(chunks):
            k = k_ref[0, rs, cs]
            inner_ref[rs, cs] = (_dot_nt(q_ref[0, rs, cs], k) * cst_ref[hd, 0]).astype(bf16)
            kz = (k.astype(f32) * cst_ref[hd, 2]).T.astype(bf16)
            kv_ref[ci, :, cs] = _dot(kz, v_ref[0, rs, cs])

    for hd, cs in enumerate(heads):
        state = jnp.zeros((RET_HEAD_DIM, RET_HEAD_DIM), f32)
        for ci in range(len(chunks)):
            st_ref[ci, :, cs] = state.astype(bf16)
            state = state * cst_ref[hd, 3] + kv_ref[ci, :, cs]

    for hd, cs in enumerate(heads):
        for ci, rs in enumerate(chunks):
            q_xi = (q_ref[0, rs, cs].astype(f32) * cst_ref[hd, 1]).astype(bf16)
            lhs = jnp.concatenate([inner_ref[rs, cs], q_xi], axis=1)
            rhs = jnp.concatenate([v_ref[0, rs, cs], st_ref[ci, :, cs]], axis=0)
            o_ref[0, rs, cs] = _dot(lhs, rhs).astype(bf16)


def _retention_constants():
    c = RET_CHUNK
    f = np.float32
    log_g = np.log(f(1.0) - f(2.0) ** (f(-5.0) - np.arange(RET_HEADS, dtype=f)))
    n = np.arange(c, dtype=f)
    diff = n[:, None] - n[None, :]
    mask = np.where(diff[None] >= 0, np.exp(np.maximum(diff, f(0.0))[None] * log_g[:, None, None]), f(0.0))
    xi = np.exp((n[None, :] + f(1.0)) * log_g[:, None])
    zeta = np.exp((f(c) - f(1.0) - n[None, :]) * log_g[:, None])
    g_chunk = np.exp(f(c) * log_g)
    bc = lambda rows: np.broadcast_to(rows[:, :, None], (RET_HEADS, c, c))
    return np.stack([mask, bc(xi), bc(zeta), np.broadcast_to(g_chunk[:, None, None], (RET_HEADS, c, c))], axis=1).astype(f)


def _retention(rq, rk, rv):
    b, s, _ = rq.shape
    cst = _retention_constants()
    seq = pl.BlockSpec((1, s, RET_WIDTH), lambda i: (i, 0, 0))
    return pl.pallas_call(
        _ret_kernel,
        grid=(b,),
        in_specs=[seq, seq, seq,
                  pl.BlockSpec(cst.shape, lambda i: (0, 0, 0, 0))],
        out_specs=seq,
        out_shape=jax.ShapeDtypeStruct((b, s, RET_WIDTH), bf16),
        scratch_shapes=[
            pltpu.VMEM((s, RET_WIDTH), bf16),
            pltpu.VMEM((s // RET_CHUNK, RET_HEAD_DIM, RET_WIDTH), f32),
            pltpu.VMEM((s // RET_CHUNK, RET_HEAD_DIM, RET_WIDTH), bf16),
        ],
        compiler_params=pltpu.CompilerParams(
            dimension_semantics=("arbitrary",), vmem_limit_bytes=VMEM_LIMIT),
        name="retention",
    )(rq, rk, rv, cst)


def _out_mlp_kernel(x_ref, af_ref, og_ref, ar_ref, rg_ref, mod_ref, fg_ref, rgain_ref, wo_ref, w1_ref, w2_ref, o_ref):
    gate_m = mod_ref[0, 2:3, :]
    shift_f = mod_ref[0, 3:4, :]
    scale_f = mod_ref[0, 4:5, :]
    gate_f = mod_ref[0, 5:6, :]
    d_ff = w1_ref.shape[1]
    fc = min(FF_CHUNK, d_ff)
    lo = _half_lane_mask()
    tm = x_ref.shape[1]
    sub = min(TOKEN_TILE, tm)
    for r in range(tm // sub):
        rs = slice(r * sub, (r + 1) * sub)
        att = af_ref[0, rs, :].astype(f32)
        normed = []
        for g in range(FOX_WIDTH // LANES):
            v = att[:, g * LANES:(g + 1) * LANES]
            normed.append(v * lax.rsqrt(_pair_sumsq(v, lo) * (1.0 / FOX_HEAD_DIM) + EPS))
        mixed_fox = (jnp.concatenate(normed, axis=-1) * fg_ref[...] * og_ref[0, rs, :].astype(f32)).astype(bf16)
        ret = ar_ref[0, rs, :].astype(f32)
        normed = []
        for g in range(RET_HEADS):
            v = ret[:, g * RET_HEAD_DIM:(g + 1) * RET_HEAD_DIM]
            normed.append(v * lax.rsqrt(jnp.mean(v * v, axis=-1, keepdims=True) + EPS))
        mixed_ret = (jnp.concatenate(normed, axis=-1) * rgain_ref[...] * rg_ref[0, rs, :].astype(f32)).astype(bf16)
        mixed = _dot(mixed_fox, wo_ref[0:FOX_WIDTH, :]) + _dot(mixed_ret, wo_ref[FOX_WIDTH:, :])
        x1 = x_ref[0, rs, :] + gate_m * mixed
        inv = lax.rsqrt(jnp.mean(x1 * x1, axis=-1, keepdims=True) + EPS)
        h = (x1 * inv * (1.0 + scale_f) + shift_f).astype(bf16)
        y = jnp.zeros(x1.shape, f32)
        for j in range(d_ff // fc):
            u = jnp.maximum(_dot(h, w1_ref[:, j * fc:(j + 1) * fc]), 0.0)
            y = y + _dot((u * u).astype(bf16), w2_ref[j * fc:(j + 1) * fc, :])
        o_ref[0, rs, :] = x1 + gate_f * y


def _out_mlp(x, attn_fox, fog, attn_ret, rg, mod, fox_gain, ret_gain, w_out, w_mlp_in, w_mlp_out):
    b, s, d = x.shape
    tm = min(2 * TOKEN_TILE, s)
    wo = w_out.astype(bf16)
    w1 = w_mlp_in.astype(bf16)
    w2 = w_mlp_out.astype(bf16)
    fg = fox_gain.reshape(1, FOX_WIDTH)
    rgain = ret_gain.reshape(1, RET_WIDTH)
    tok = lambda width: pl.BlockSpec((1, tm, width), lambda i, j: (i, j, 0))
    resident = lambda a: pl.BlockSpec(a.shape, lambda i, j: (0, 0), pipeline_mode=pl.Buffered(1))
    return pl.pallas_call(
        _out_mlp_kernel,
        grid=(b, s // tm),
        in_specs=[
            tok(d), tok(FOX_WIDTH), tok(FOX_WIDTH), tok(RET_WIDTH), tok(RET_WIDTH),
            pl.BlockSpec((1, N_MOD, d), lambda i, j: (i, 0, 0)),
            resident(fg), resident(rgain), resident(wo), resident(w1), resident(w2),
        ],
        out_specs=tok(d),
        out_shape=jax.ShapeDtypeStruct((b, s, d), f32),
        compiler_params=pltpu.CompilerParams(
            dimension_semantics=("arbitrary", "arbitrary"), vmem_limit_bytes=VMEM_LIMIT),
        name="out_mlp",
    )(x, attn_fox, fog, attn_ret, rg, mod, fg, rgain, wo, w1, w2)


def kernel(x, c, w_ada, b_ada, w_in, b_forget, q_norm_gain, k_norm_gain, fox_out_gain, ret_out_gain,
           w_out, w_mlp_in, w_mlp_out):
    b, s, d = x.shape
    for l in range(w_ada.shape[0]):
        mod = _modulation(c, w_ada[l], b_ada[l]).reshape(b, N_MOD, d)
        fq, fk, fv, fog, log_f, rq, rk, rv, rg = _input_projection(
            x, mod, w_in[l], b_forget[l], q_norm_gain[l], k_norm_gain[l])
        attn_fox = _fox_attention(fq, fk, fv, log_f)
        attn_ret = _retention(rq, rk, rv)
        x = _out_mlp(x, attn_fox, fog, attn_ret, rg, mod, fox_out_gain[l], ret_out_gain[l],
                     w_out[l], w_mlp_in[l], w_mlp_out[l])
    return x
```

```python
import numpy as np

import jax
import jax.numpy as jnp
from jax import lax
from jax.experimental import pallas as pl
from jax.experimental.pallas import tpu as pltpu

FOX_HEADS = 8
FOX_HEAD_DIM = 64
FOX_WIDTH = FOX_HEADS * FOX_HEAD_DIM
RET_HEADS = 4
RET_HEAD_DIM = 128
RET_WIDTH = RET_HEADS * RET_HEAD_DIM
RET_CHUNK = 128
ROPE_BASE = 10000.0
EPS = 1e-6
N_MOD = 6

LANES = 128
VMEM_LIMIT = 56 * 1024 * 1024
MASK_VALUE = -1e30
LOG2E = 1.4426950408889634

TOKEN_TILE = 512
ATTN_TILE = 256
FF_CHUNK = 1024
RET_SEQS_PER_STEP = 2

bf16 = jnp.bfloat16
f32 = jnp.float32


def _dot(a, b):
    return jnp.dot(a, b, preferred_element_type=f32)


def _dot_nt(a, b):
    return lax.dot_general(a, b, (((1,), (1,)), ((), ())), preferred_element_type=f32)


def _half_lane_mask():
    return lax.broadcasted_iota(jnp.int32, (1, LANES), 1) < FOX_HEAD_DIM


def _pair_sumsq(v, lo):
    sq = v * v
    s_lo = jnp.sum(jnp.where(lo, sq, 0.0), axis=-1, keepdims=True)
    s_hi = jnp.sum(jnp.where(lo, 0.0, sq), axis=-1, keepdims=True)
    return jnp.where(lo, s_lo, s_hi)


def _mod_kernel(c_ref, w_ref, b_ref, o_ref):
    c = c_ref[...]
    c_act = (c * jax.nn.sigmoid(c)).astype(bf16)
    o_ref[...] = _dot(c_act, w_ref[...].astype(bf16)) + b_ref[...]


def _modulation(c, w_ada, b_ada):
    b, d = c.shape
    n = w_ada.shape[1]
    return pl.pallas_call(
        _mod_kernel,
        grid=(n // d,),
        in_specs=[
            pl.BlockSpec((b, d), lambda j: (0, 0)),
            pl.BlockSpec((d, d), lambda j: (0, j)),
            pl.BlockSpec((1, d), lambda j: (0, j)),
        ],
        out_specs=pl.BlockSpec((b, d), lambda j: (0, j)),
        out_shape=jax.ShapeDtypeStruct((b, n), f32),
        compiler_params=pltpu.CompilerParams(
            dimension_semantics=("arbitrary",), vmem_limit_bytes=VMEM_LIMIT),
        name="adaln_mod",
    )(c, w_ada, b_ada.reshape(1, n))


def _inproj_kernel(x_ref, mod_ref, wf_ref, wg_ref, wr_ref, gq_ref, gk_ref, bf_ref, cos_ref, sin_ref,
                   fq_ref, fk_ref, fv_ref, fog_ref, lf_ref, rq_ref, rk_ref, rv_ref, rg_ref):
    lo = _half_lane_mask()
    w = FOX_WIDTH
    rw = RET_WIDTH
    tm = x_ref.shape[1]
    sub = min(TOKEN_TILE, tm)

    def qk_norm(p, gain_ref):
        outs = []
        for g in range(w // LANES):
            v = p[:, g * LANES:(g + 1) * LANES]
            ss = _pair_sumsq(v, lo)
            outs.append(v * lax.rsqrt(ss * (1.0 / FOX_HEAD_DIM) + EPS))
        return (jnp.concatenate(outs, axis=-1) * gain_ref[...]).astype(bf16)

    def rope(p, cos, sin, scale):
        outs = []
        for g in range(RET_HEADS):
            v = p[:, g * LANES:(g + 1) * LANES]
            outs.append(v * cos + pltpu.roll(v, RET_HEAD_DIM // 2, 1) * sin)
        r = jnp.concatenate(outs, axis=-1)
        return (r * scale if scale != 1.0 else r).astype(bf16)

    for r in range(tm // sub):
        rs = slice(r * sub, (r + 1) * sub)
        x = x_ref[0, rs, :]
        inv = lax.rsqrt(jnp.mean(x * x, axis=-1, keepdims=True) + EPS)
        h = (x * inv * (1.0 + mod_ref[0, 1:2, :]) + mod_ref[0, 0:1, :]).astype(bf16)

        def proj(w_ref, start, n):
            return _dot(h, w_ref[:, start:start + n])

        fq_ref[0, rs, :] = qk_norm(proj(wf_ref, 0, w), gq_ref)
        fk_ref[0, rs, :] = qk_norm(proj(wf_ref, w, w), gk_ref)
        fog_ref[0, rs, :] = jax.nn.sigmoid(proj(wf_ref, 3 * w, w)).astype(bf16)

        z = proj(wg_ref, 0, LANES) + bf_ref[...]
        log_f = jnp.minimum(z, 0.0) - jnp.log(1.0 + jnp.exp(-jnp.abs(z)))
        lf_ref[0, :, rs] = log_f.T[0:FOX_HEADS, :]

        cos = cos_ref[rs, :]
        sin = sin_ref[rs, :]
        rq_ref[0, rs, :] = rope(proj(wr_ref, 0, rw), cos, sin, 1.0)
        rk_ref[0, rs, :] = rope(proj(wr_ref, rw, rw), cos, sin, RET_HEAD_DIM ** -0.5)
        gate = proj(wr_ref, 3 * rw, rw)
        rg_ref[0, rs, :] = (gate * jax.nn.sigmoid(gate)).astype(bf16)
        rv_ref[0, rs, :] = proj(wr_ref, 2 * rw, rw).astype(bf16)
        fv_ref[0, rs, :] = proj(wf_ref, 2 * w, w).astype(bf16)


def _input_projection(x, mod, w_in, b_forget, q_gain, k_gain):
    b, s, d = x.shape
    tm = min(2 * TOKEN_TILE, s)
    o_ff = 4 * FOX_WIDTH
    o_r = o_ff + FOX_HEADS
    w_fox = w_in[:, :o_ff].astype(bf16)
    w_fg = jnp.pad(w_in[:, o_ff:o_r], ((0, 0), (0, LANES - FOX_HEADS))).astype(bf16)
    w_ret = w_in[:, o_r:].astype(bf16)
    bias_f = jnp.pad(b_forget, (0, LANES - FOX_HEADS)).reshape(1, LANES)
    gq = (jnp.tile(q_gain, FOX_HEADS) * (LOG2E * FOX_HEAD_DIM ** -0.5)).reshape(1, FOX_WIDTH)
    gk = jnp.tile(k_gain, FOX_HEADS).reshape(1, FOX_WIDTH)

    pos = np.arange(s, dtype=np.float32)
    inv_freq = np.float32(ROPE_BASE) ** (-np.arange(0, RET_HEAD_DIM, 2, dtype=np.float32) / np.float32(RET_HEAD_DIM))
    ang = pos[:, None] * inv_freq[None, :]
    cos_t = np.concatenate([np.cos(ang), np.cos(ang)], axis=-1).astype(np.float32)
    sin_t = np.concatenate([-np.sin(ang), np.sin(ang)], axis=-1).astype(np.float32)

    tok = lambda width: pl.BlockSpec((1, tm, width), lambda i, j: (i, j, 0))
    full = lambda a: pl.BlockSpec(a.shape, lambda i, j: (0,) * a.ndim, pipeline_mode=pl.Buffered(1))
    act = lambda width: jax.ShapeDtypeStruct((b, s, width), bf16)
    return pl.pallas_call(
        _inproj_kernel,
        grid=(b, s // tm),
        in_specs=[
            tok(d),
            pl.BlockSpec((1, N_MOD, d), lambda i, j: (i, 0, 0)),
            full(w_fox), full(w_fg), full(w_ret), full(gq), full(gk), full(bias_f),
            pl.BlockSpec((tm, LANES), lambda i, j: (j, 0)),
            pl.BlockSpec((tm, LANES), lambda i, j: (j, 0)),
        ],
        out_specs=[
            tok(FOX_WIDTH), tok(FOX_WIDTH), tok(FOX_WIDTH), tok(FOX_WIDTH),
            pl.BlockSpec((1, FOX_HEADS, tm), lambda i, j: (i, 0, j)),
            tok(RET_WIDTH), tok(RET_WIDTH), tok(RET_WIDTH), tok(RET_WIDTH),
        ],
        out_shape=[
            act(FOX_WIDTH), act(FOX_WIDTH), act(FOX_WIDTH), act(FOX_WIDTH),
            jax.ShapeDtypeStruct((b, FOX_HEADS, s), f32),
            act(RET_WIDTH), act(RET_WIDTH), act(RET_WIDTH), act(RET_WIDTH),
        ],
        compiler_params=pltpu.CompilerParams(
            dimension_semantics=("arbitrary", "arbitrary"), vmem_limit_bytes=VMEM_LIMIT),
        name="in_proj",
    )(x, mod, w_fox, w_fg, w_ret, gq, gk, bias_f, cos_t, sin_t)


def _lane_cumsum(x, out_ref):
    rows, s = x.shape
    r = lax.broadcasted_iota(jnp.int32, (LANES, LANES), 0)
    c = lax.broadcasted_iota(jnp.int32, (LANES, LANES), 1)
    tri = (r <= c).astype(bf16)
    hi = x.astype(bf16).astype(f32)
    rest = x - hi
    mid = rest.astype(bf16).astype(f32)
    low = (rest - mid).astype(bf16).astype(f32)
    n = s // LANES
    chunks = [slice(ch * LANES, (ch + 1) * LANES) for ch in range(n)]
    pieces = jnp.concatenate([g[:, sl] for g in (hi, mid, low) for sl in chunks], axis=0)
    rhs = jnp.concatenate([tri, jnp.ones((LANES, LANES), bf16)], axis=1)
    part = _dot(pieces.astype(bf16), rhs)
    offset = jnp.zeros((rows, LANES), f32)
    for ch, sl in enumerate(chunks):
        hi_c, mid_c, low_c = (part[(g * n + ch) * rows:(g * n + ch + 1) * rows] for g in range(3))
        both = hi_c + mid_c + low_c
        out_ref[:, sl] = both[:, 0:LANES] + offset
        offset = offset + both[:, LANES:]


def _fox_kernel(q_ref, k_ref, v_ref, lf_ref, o_ref, cum_ref, dec_ref, s_ref):
    pair = pl.program_id(1)
    s = q_ref.shape[1]
    t = min(ATTN_TILE, s)

    @pl.when(pair == 0)
    def _():
        _lane_cumsum(lf_ref[0] * LOG2E, cum_ref)
        neg = -cum_ref[...]
        hi = neg.astype(bf16).astype(f32)
        mid = (neg - hi).astype(bf16).astype(f32)
        low = (neg - hi - mid).astype(bf16).astype(f32)
        sub = lax.broadcasted_iota(jnp.int32, (3 * FOX_HEADS, 1), 0)
        decay_t = jnp.zeros((3 * FOX_HEADS, s), f32)
        for h in range(FOX_HEADS):
            for g, piece in enumerate((hi, mid, low)):
                decay_t = jnp.where(sub == 3 * h + g, piece[h:h + 1], decay_t)
        decay_t = jnp.concatenate([decay_t, jnp.zeros((LANES - 3 * FOX_HEADS, s), f32)], axis=0)
        dec_ref[...] = decay_t.T.astype(bf16)

    lo = _half_lane_mask()
    row = lax.broadcasted_iota(jnp.int32, (t, t), 0)
    col = lax.broadcasted_iota(jnp.int32, (t, t), 1)
    causal = row >= col
    lane = lax.broadcasted_iota(jnp.int32, (1, LANES), 1)
    first = 6 * pair
    ones_a = jnp.where((lane >= first) & (lane < first + 3), 1.0, 0.0).astype(bf16)
    ones_b = jnp.where((lane >= first + 3) & (lane < first + 6), 1.0, 0.0).astype(bf16)
    nq = s // t
    row_max, pv_acc = {}, {}

    def score_pass(qi):
        q = q_ref[0, qi * t:(qi + 1) * t, :]
        zero = jnp.zeros_like(q)
        q2 = jnp.concatenate(
            [jnp.concatenate([jnp.where(lo, q, zero), jnp.broadcast_to(ones_a, q.shape)], axis=1),
             jnp.concatenate([jnp.where(lo, zero, q), jnp.broadcast_to(ones_b, q.shape)], axis=1)], axis=0)
        m_run = jnp.full((2 * t, LANES), MASK_VALUE, f32)
        for kb in range(qi + 1):
            ks = slice(kb * t, (kb + 1) * t)
            keys = jnp.concatenate([k_ref[0, ks, :], dec_ref[ks, :]], axis=1)
            sc = _dot_nt(q2, keys)
            if kb == qi:
                sc = jnp.concatenate([jnp.where(causal, sc[0:t], MASK_VALUE),
                                      jnp.where(causal, sc[t:], MASK_VALUE)], axis=0)
            s_ref[qi % 2, :, ks] = sc
            for j in range(t // LANES):
                m_run = jnp.maximum(m_run, sc[:, j * LANES:(j + 1) * LANES])
            yield
        row_max[qi] = jnp.max(m_run, axis=-1, keepdims=True)

    def prob_pass(qi):
        m = row_max.pop(qi)
        acc = None
        for kb in range(qi + 1):
            ks = slice(kb * t, (kb + 1) * t)
            p = jnp.exp2(s_ref[qi % 2, :, ks] - m).astype(bf16)
            v1 = jnp.concatenate([v_ref[0, ks, :], jnp.ones((t, LANES), bf16)], axis=1)
            part = _dot(p, v1)
            acc = part if acc is None else acc + part
            yield
        pv_acc[qi] = acc

    def finish(qi):
        rows = slice(qi * t, (qi + 1) * t)
        pv = pv_acc.pop(qi)
        o2 = pv[:, 0:LANES] / pv[:, LANES:]
        o_ref[0, rows, :] = jnp.where(lo, o2[0:t], o2[t:]).astype(bf16)

    def interleave(*passes):
        live = list(passes)
        while live:
            live = [g for g in live if next(g, StopIteration) is not StopIteration]

    interleave(score_pass(nq - 1))
    for qi in reversed(range(nq)):
        if qi + 1 < nq:
            finish(qi + 1)
        interleave(*([score_pass(qi - 1)] if qi >= 1 else []), prob_pass(qi))
    finish(0)


def _fox_attention(fq, fk, fv, log_f):
    b, s, _ = fq.shape
    t = min(ATTN_TILE, s)
    pairs = FOX_WIDTH // LANES
    seq = pl.BlockSpec((1, s, LANES), lambda i, p: (i, 0, p))
    return pl.pallas_call(
        _fox_kernel,
        grid=(b, pairs),
        in_specs=[
            seq, seq, seq,
            pl.BlockSpec((1, FOX_HEADS, s), lambda i, p: (i, 0, 0)),
        ],
        out_specs=seq,
        out_shape=jax.ShapeDtypeStruct((b, s, FOX_WIDTH), bf16),
        scratch_shapes=[
            pltpu.VMEM((FOX_HEADS, s), f32),
            pltpu.VMEM((s, LANES), bf16),
            pltpu.VMEM((2, 2 * t, s), f32),
        ],
        compiler_params=pltpu.CompilerParams(
            dimension_semantics=("arbitrary", "arbitrary"), vmem_limit_bytes=VMEM_LIMIT),
        name="fox_attention",
    )(fq, fk, fv, log_f)


def _ret_kernel(q_ref, k_ref, v_ref, cst_ref, o_ref, inner_ref, kv_ref, st_ref):
    s = q_ref.shape[1]
    c = RET_CHUNK
    heads = [slice(hd * RET_HEAD_DIM, (hd + 1) * RET_HEAD_DIM) for hd in range(RET_HEADS)]
    chunks = [slice(ci * c, (ci + 1) * c) for ci in range(s // c)]

    for bi in range(q_ref.shape[0]):
        for hd, cs in enumerate(heads):
            for ci, rs in enumerate(chunks):
                k = k_ref[bi, rs, cs]
                inner_ref[rs, cs] = (_dot_nt(q_ref[bi, rs, cs], k) * cst_ref[hd, 0]).astype(bf16)
                kz = (k.astype(f32) * cst_ref[hd, 2]).T.astype(bf16)
                kv_ref[ci, :, cs] = _dot(kz, v_ref[bi, rs, cs])

        for hd, cs in enumerate(heads):
            state = jnp.zeros((RET_HEAD_DIM, RET_HEAD_DIM), f32)
            for ci in range(len(chunks)):
                st_ref[ci, :, cs] = state.astype(bf16)
                state = state * cst_ref[hd, 3] + kv_ref[ci, :, cs]

        for hd, cs in enumerate(heads):
            for ci, rs in enumerate(chunks):
                q_xi = (q_ref[bi, rs, cs].astype(f32) * cst_ref[hd, 1]).astype(bf16)
                lhs = jnp.concatenate([inner_ref[rs, cs], q_xi], axis=1)
                rhs = jnp.concatenate([v_ref[bi, rs, cs], st_ref[ci, :, cs]], axis=0)
                o_ref[bi, rs, cs] = _dot(lhs, rhs).astype(bf16)


def _retention_constants():
    c = RET_CHUNK
    f = np.float32
    log_g = np.log(f(1.0) - f(2.0) ** (f(-5.0) - np.arange(RET_HEADS, dtype=f)))
    n = np.arange(c, dtype=f)
    diff = n[:, None] - n[None, :]
    mask = np.where(diff[None] >= 0, np.exp(np.maximum(diff, f(0.0))[None] * log_g[:, None, None]), f(0.0))
    xi = np.exp((n[None, :] + f(1.0)) * log_g[:, None])
    zeta = np.exp((f(c) - f(1.0) - n[None, :]) * log_g[:, None])
    g_chunk = np.exp(f(c) * log_g)
    bc = lambda rows: np.broadcast_to(rows[:, :, None], (RET_HEADS, c, c))
    return np.stack([mask, bc(xi), bc(zeta), np.broadcast_to(g_chunk[:, None, None], (RET_HEADS, c, c))], axis=1).astype(f)


def _retention(rq, rk, rv):
    b, s, _ = rq.shape
    cst = _retention_constants()
    nb = RET_SEQS_PER_STEP if b % RET_SEQS_PER_STEP == 0 else 1
    seq = pl.BlockSpec((nb, s, RET_WIDTH), lambda i: (i, 0, 0))
    return pl.pallas_call(
        _ret_kernel,
        grid=(b // nb,),
        in_specs=[seq, seq, seq,
                  pl.BlockSpec(cst.shape, lambda i: (0, 0, 0, 0))],
        out_specs=seq,
        out_shape=jax.ShapeDtypeStruct((b, s, RET_WIDTH), bf16),
        scratch_shapes=[
            pltpu.VMEM((s, RET_WIDTH), bf16),
            pltpu.VMEM((s // RET_CHUNK, RET_HEAD_DIM, RET_WIDTH), f32),
            pltpu.VMEM((s // RET_CHUNK, RET_HEAD_DIM, RET_WIDTH), bf16),
        ],
        compiler_params=pltpu.CompilerParams(
            dimension_semantics=("arbitrary",), vmem_limit_bytes=VMEM_LIMIT),
        name="retention",
    )(rq, rk, rv, cst)


def _out_mlp_kernel(x_ref, af_ref, og_ref, ar_ref, rg_ref, mod_ref, fg_ref, rgain_ref, wo_ref, w1_ref, w2_ref, o_ref):
    gate_m = mod_ref[0, 2:3, :]
    shift_f = mod_ref[0, 3:4, :]
    scale_f = mod_ref[0, 4:5, :]
    gate_f = mod_ref[0, 5:6, :]
    d_ff = w1_ref.shape[1]
    fc = min(FF_CHUNK, d_ff)
    lo = _half_lane_mask()
    tm = x_ref.shape[1]
    sub = min(TOKEN_TILE, tm)
    for r in range(tm // sub):
        rs = slice(r * sub, (r + 1) * sub)
        att = af_ref[0, rs, :].astype(f32)
        normed = []
        for g in range(FOX_WIDTH // LANES):
            v = att[:, g * LANES:(g + 1) * LANES]
            normed.append(v * lax.rsqrt(_pair_sumsq(v, lo) * (1.0 / FOX_HEAD_DIM) + EPS))
        mixed_fox = (jnp.concatenate(normed, axis=-1) * fg_ref[...] * og_ref[0, rs, :].astype(f32)).astype(bf16)
        ret = ar_ref[0, rs, :].astype(f32)
        normed = []
        for g in range(RET_HEADS):
            v = ret[:, g * RET_HEAD_DIM:(g + 1) * RET_HEAD_DIM]
            normed.append(v * lax.rsqrt(jnp.mean(v * v, axis=-1, keepdims=True) + EPS))
        mixed_ret = (jnp.concatenate(normed, axis=-1) * rgain_ref[...] * rg_ref[0, rs, :].astype(f32)).astype(bf16)
        mixed = _dot(mixed_fox, wo_ref[0:FOX_WIDTH, :]) + _dot(mixed_ret, wo_ref[FOX_WIDTH:, :])
        x1 = x_ref[0, rs, :] + gate_m * mixed
        inv = lax.rsqrt(jnp.mean(x1 * x1, axis=-1, keepdims=True) + EPS)
        h = (x1 * inv * (1.0 + scale_f) + shift_f).astype(bf16)
        y = jnp.zeros(x1.shape, f32)
        for j in range(d_ff // fc):
            u = jnp.maximum(_dot(h, w1_ref[:, j * fc:(j + 1) * fc]), 0.0)
            y = y + _dot((u * u).astype(bf16), w2_ref[j * fc:(j + 1) * fc, :])
        o_ref[0, rs, :] = x1 + gate_f * y


def _out_mlp(x, attn_fox, fog, attn_ret, rg, mod, fox_gain, ret_gain, w_out, w_mlp_in, w_mlp_out):
    b, s, d = x.shape
    tm = min(2 * TOKEN_TILE, s)
    wo = w_out.astype(bf16)
    w1 = w_mlp_in.astype(bf16)
    w2 = w_mlp_out.astype(bf16)
    fg = fox_gain.reshape(1, FOX_WIDTH)
    rgain = ret_gain.reshape(1, RET_WIDTH)
    tok = lambda width: pl.BlockSpec((1, tm, width), lambda i, j: (i, j, 0))
    resident = lambda a: pl.BlockSpec(a.shape, lambda i, j: (0, 0), pipeline_mode=pl.Buffered(1))
    return pl.pallas_call(
        _out_mlp_kernel,
        grid=(b, s // tm),
        in_specs=[
            tok(d), tok(FOX_WIDTH), tok(FOX_WIDTH), tok(RET_WIDTH), tok(RET_WIDTH),
            pl.BlockSpec((1, N_MOD, d), lambda i, j: (i, 0, 0)),
            resident(fg), resident(rgain), resident(wo), resident(w1), resident(w2),
        ],
        out_specs=tok(d),
        out_shape=jax.ShapeDtypeStruct((b, s, d), f32),
        compiler_params=pltpu.CompilerParams(
            dimension_semantics=("arbitrary", "arbitrary"), vmem_limit_bytes=VMEM_LIMIT),
        name="out_mlp",
    )(x, attn_fox, fog, attn_ret, rg, mod, fg, rgain, wo, w1, w2)


def kernel(x, c, w_ada, b_ada, w_in, b_forget, q_norm_gain, k_norm_gain, fox_out_gain, ret_out_gain,
           w_out, w_mlp_in, w_mlp_out):
    b, s, d = x.shape
    for l in range(w_ada.shape[0]):
        mod = _modulation(c, w_ada[l], b_ada[l]).reshape(b, N_MOD, d)
        fq, fk, fv, fog, log_f, rq, rk, rv, rg = _input_projection(
            x, mod, w_in[l], b_forget[l], q_norm_gain[l], k_norm_gain[l])
        attn_fox = _fox_attention(fq, fk, fv, log_f)
        attn_ret = _retention(rq, rk, rv)
        x = _out_mlp(x, attn_fox, fog, attn_ret, rg, mod, fox_out_gain[l], ret_out_gain[l],
                     w_out[l], w_mlp_in[l], w_mlp_out[l])
    return x
```
